```python
import math
import jax, jax.numpy as jnp
from jax import lax
import numpy as np

D_MODEL = 1024
BATCH = 4
SEQ = 8192
DEPTH = 2

N_A = DEPTH // 2
N_B = DEPTH - N_A
CHUNK = 128
GMLP_WIDTH = 2 * D_MODEL
GMLP_GROUPS = 8
GMLP_GROUP_DIM = GMLP_WIDTH // GMLP_GROUPS
DIFF_HEADS = 8
DIFF_HEAD_DIM = D_MODEL // (2 * DIFF_HEADS)
DIFF_V_DIM = 2 * DIFF_HEAD_DIM
D_K = DIFF_HEADS * 2 * DIFF_HEAD_DIM
D_V = DIFF_HEADS * DIFF_V_DIM
ROT_DIM = DIFF_HEAD_DIM // 4
ROPE_THETA = 500000.0
Q_BLOCK = 128
D_FF = -(-(8 * D_MODEL) // (3 * 256)) * 256
EPS = 1e-5

kernel_name = "yoco_gmlp_diffattn_hybrid"


def rmsnorm(x, g):
    x32 = x.astype(jnp.float32)
    y = x32 * lax.rsqrt(jnp.mean(x32 * x32, axis=-1, keepdims=True) + EPS)
    return (y * g.astype(jnp.float32)).astype(x.dtype)


def layernorm(x, g, b):
    x32 = x.astype(jnp.float32)
    mu = jnp.mean(x32, axis=-1, keepdims=True)
    xc = x32 - mu
    y = xc * lax.rsqrt(jnp.mean(xc * xc, axis=-1, keepdims=True) + EPS)
    return (y * g.astype(jnp.float32) + b.astype(jnp.float32)).astype(x.dtype)


def rope_partial(x, pos):
    half = ROT_DIM // 2
    inv_freq = jnp.power(ROPE_THETA, -jnp.arange(half, dtype=jnp.float32) * 2.0 / ROT_DIM)
    ang = pos.astype(jnp.float32)[:, None] * inv_freq[None, :]
    cos = jnp.cos(ang)[:, None, None, :]
    sin = jnp.sin(ang)[:, None, None, :]
    xr = x[..., :ROT_DIM].astype(jnp.float32)
    x1, x2 = xr[..., :half], xr[..., half:]
    rot = jnp.concatenate([x1 * cos - x2 * sin, x2 * cos + x1 * sin], axis=-1)
    return jnp.concatenate([rot.astype(x.dtype), x[..., ROT_DIM:]], axis=-1)


def gmlp_mixer(h, w_in, ln_g, ln_b, w_s, b_s, w_out):
    B, S, _ = h.shape
    z = jax.nn.gelu(h @ w_in, approximate=False)
    u, v = jnp.split(z, 2, axis=-1)
    v = layernorm(v, ln_g, ln_b)
    n_chunks = S // CHUNK
    v = v.reshape(B, n_chunks, CHUNK, GMLP_GROUPS, GMLP_GROUP_DIM)
    causal = jnp.tril(jnp.ones((CHUNK, CHUNK), dtype=bool))
    ws = jnp.where(causal[None], w_s, 0).astype(v.dtype)
    mixed = jnp.einsum('gts,bcsgd->bctgd', ws, v) + b_s.T.astype(v.dtype)[None, None, :, :, None]
    gated = u * mixed.reshape(B, S, GMLP_WIDTH)
    return gated @ w_out


def shared_kv(h, kv_norm_g, w_kv, pos):
    B, S, _ = h.shape
    kv = rmsnorm(h, kv_norm_g) @ w_kv
    k, v = jnp.split(kv, [D_K], axis=-1)
    k = rope_partial(k.reshape(B, S, DIFF_HEADS, 2, DIFF_HEAD_DIM), pos)
    v = v.reshape(B, S, DIFF_HEADS, DIFF_V_DIM)
    return k, v


def diff_attention(h, k, v, pos, w_q, lam_q, lam_k, sub_g, w_o, lam_init):
    B, S, _ = h.shape
    q = rope_partial((h @ w_q).reshape(B, S, DIFF_HEADS, 2, DIFF_HEAD_DIM), pos)
    lq = lam_q.astype(jnp.float32)
    lk = lam_k.astype(jnp.float32)
    lam = jnp.exp(jnp.sum(lq[0] * lk[0])) - jnp.exp(jnp.sum(lq[1] * lk[1])) + lam_init
    n_blocks = S // Q_BLOCK
    qb = q.reshape(B, n_blocks, Q_BLOCK, DIFF_HEADS, 2, DIFF_HEAD_DIM).transpose(1, 0, 2, 3, 4, 5)
    k_pos = jnp.arange(S, dtype=jnp.int32)
    scale = DIFF_HEAD_DIM ** -0.5

    def block(args):
        q_blk, i = args
        q_pos = i * Q_BLOCK + jnp.arange(Q_BLOCK, dtype=jnp.int32)
        s = jnp.einsum('bqhcd,bkhcd->bhcqk', q_blk, k).astype(jnp.float32) * scale
        s = jnp.where(k_pos[None, :] <= q_pos[:, None], s, -jnp.inf)
        p = jax.nn.softmax(s, axis=-1)
        a = p[:, :, 0] - lam * p[:, :, 1]
        return jnp.einsum('bhqk,bkhe->bqhe', a.astype(v.dtype), v)

    o = lax.map(block, (qb, jnp.arange(n_blocks, dtype=jnp.int32)))
    o = o.transpose(1, 0, 2, 3, 4).reshape(B, S, DIFF_HEADS, DIFF_V_DIM)
    o = rmsnorm(o, sub_g) * (1.0 - lam_init)
    return o.reshape(B, S, D_V) @ w_o


def swiglu(h, w_gu, w_down):
    gate, up = jnp.split(h @ w_gu, 2, axis=-1)
    return (jax.nn.silu(gate) * up) @ w_down


def setup_inputs(seed: int = 0) -> dict:
    key = jax.random.key(seed)
    ks = jax.random.split(key, 20)
    f32 = jnp.float32
    nrm = lambda k, shape, s: jax.random.normal(k, shape, f32) * s
    return {
        "x": nrm(ks[0], (BATCH, SEQ, D_MODEL), 1.0),
        "attn_norm_g": 1.0 + nrm(ks[1], (DEPTH, D_MODEL), 0.02),
        "ffn_norm_g": 1.0 + nrm(ks[2], (DEPTH, D_MODEL), 0.02),
        "gmlp_w_in": nrm(ks[3], (N_A, D_MODEL, 2 * GMLP_WIDTH), D_MODEL ** -0.5),
        "gmlp_ln_g": 1.0 + nrm(ks[4], (N_A, GMLP_WIDTH), 0.02),
        "gmlp_ln_b": nrm(ks[5], (N_A, GMLP_WIDTH), 0.02),
        "gmlp_w_s": nrm(ks[6], (N_A, GMLP_GROUPS, CHUNK, CHUNK), CHUNK ** -0.5),
        "gmlp_b_s": 1.0 + nrm(ks[7], (N_A, GMLP_GROUPS, CHUNK), 0.1),
        "gmlp_w_out": nrm(ks[8], (N_A, GMLP_WIDTH, D_MODEL), GMLP_WIDTH ** -0.5),
        "kv_norm_g": 1.0 + nrm(ks[9], (D_MODEL,), 0.02),
        "w_kv": nrm(ks[10], (D_MODEL, D_K + D_V), D_MODEL ** -0.5),
        "diff_w_q": nrm(ks[11], (N_B, D_MODEL, D_K), D_MODEL ** -0.5),
        "diff_lambda_q": nrm(ks[12], (N_B, 2, DIFF_HEAD_DIM), 0.1),
        "diff_lambda_k": nrm(ks[13], (N_B, 2, DIFF_HEAD_DIM), 0.1),
        "diff_sub_g": 1.0 + nrm(ks[14], (N_B, DIFF_V_DIM), 0.02),
        "diff_w_o": nrm(ks[15], (N_B, D_V, D_MODEL), D_V ** -0.5),
        "ffn_w_gu": nrm(ks[16], (DEPTH, D_MODEL, 2 * D_FF), D_MODEL ** -0.5),
        "ffn_w_down": nrm(ks[17], (DEPTH, D_FF, D_MODEL), D_FF ** -0.5),
        "final_norm_g": 1.0 + nrm(ks[18], (D_MODEL,), 0.02),
    }


def reference(x, attn_norm_g, ffn_norm_g, gmlp_w_in, gmlp_ln_g, gmlp_ln_b, gmlp_w_s, gmlp_b_s,
              gmlp_w_out, kv_norm_g, w_kv, diff_w_q, diff_lambda_q, diff_lambda_k, diff_sub_g,
              diff_w_o, ffn_w_gu, ffn_w_down, final_norm_g):
    S = x.shape[1]
    pos = jnp.arange(S, dtype=jnp.int32)
    h = x
    k_sh, v_sh = None, None
    for l in range(DEPTH):
        hn = rmsnorm(h, attn_norm_g[l])
        if l < N_A:
            a = l
            h = h + gmlp_mixer(hn, gmlp_w_in[a], gmlp_ln_g[a], gmlp_ln_b[a], gmlp_w_s[a],
                               gmlp_b_s[a], gmlp_w_out[a])
        else:
            b = l - N_A
            lam_init = 0.8 - 0.6 * math.exp(-0.3 * l)
            h = h + diff_attention(hn, k_sh, v_sh, pos, diff_w_q[b], diff_lambda_q[b],
                                   diff_lambda_k[b], diff_sub_g[b], diff_w_o[b], lam_init)
        h = h + swiglu(rmsnorm(h, ffn_norm_g[l]), ffn_w_gu[l], ffn_w_down[l])
        if l == N_A - 1:
            k_sh, v_sh = shared_kv(h, kv_norm_g, w_kv, pos)
    return rmsnorm(h, final_norm_g)
```

```python
import functools
import math

import jax
import jax.numpy as jnp
from jax import lax
from jax.experimental import pallas as pl
from jax.experimental.pallas import tpu as pltpu

D_MODEL = 1024
CHUNK = 128
GMLP_WIDTH = 2 * D_MODEL
GMLP_GROUPS = 8
GMLP_GROUP_DIM = GMLP_WIDTH // GMLP_GROUPS
DIFF_HEADS = 8
DIFF_HEAD_DIM = D_MODEL // (2 * DIFF_HEADS)
DIFF_V_DIM = 2 * DIFF_HEAD_DIM
D_K = DIFF_HEADS * 2 * DIFF_HEAD_DIM
D_V = DIFF_HEADS * DIFF_V_DIM
ROT_DIM = DIFF_HEAD_DIM // 4
ROPE_THETA = 500000.0
D_FF = -(-(8 * D_MODEL) // (3 * 256)) * 256
EPS = 1e-5
N_A = 1

LANES = 128
VMEM_LIMIT_BYTES = 56 * 2**20

TM_GMLP = 256
TM_FFN = 256
TM_QKV = 512
TQ = 256
TK = 256

F32 = jnp.float32
BF16 = jnp.bfloat16


def _rms_scale(x):
    return x * lax.rsqrt(jnp.mean(x * x, axis=-1, keepdims=True) + EPS)


def _gelu(x):
    return 0.5 * x * (1.0 + lax.erf(x * math.sqrt(0.5)))


def _resident(shape):
    return pl.BlockSpec(shape, lambda *_: (0,) * len(shape), pipeline_mode=pl.Buffered(1))


def _params():
    return pltpu.CompilerParams(vmem_limit_bytes=VMEM_LIMIT_BYTES)


def _gmlp_kernel(x_ref, g_ref, win_ref, lng_ref, lnb_ref, ws_ref, bs_ref, wout_ref, o_ref,
                 vn_ref, gated_ref):
    x = x_ref[...]
    xn = (_rms_scale(x) * g_ref[...]).astype(BF16)

    v = jnp.dot(xn, win_ref[:, GMLP_WIDTH:], preferred_element_type=F32)
    v = _gelu(v)
    mu = jnp.mean(v, axis=-1, keepdims=True)
    vc = v - mu
    vn = vc * lax.rsqrt(jnp.mean(vc * vc, axis=-1, keepdims=True) + EPS)
    vn_ref[...] = (vn * lng_ref[...] + lnb_ref[...]).astype(BF16)

    row = lax.broadcasted_iota(jnp.int32, (CHUNK, CHUNK), 0)
    col = lax.broadcasted_iota(jnp.int32, (CHUNK, CHUNK), 1)
    causal = col <= row
    n_chunks = x.shape[0] // CHUNK
    for g in range(GMLP_GROUPS):
        cols = slice(g * GMLP_GROUP_DIM, (g + 1) * GMLP_GROUP_DIM)
        u = jnp.dot(xn, win_ref[:, cols], preferred_element_type=F32)
        u = _gelu(u)
        ws = jnp.where(causal, ws_ref[g], 0.0).astype(BF16)
        bias = bs_ref[:, g:g + 1]
        for c in range(n_chunks):
            rows = slice(c * CHUNK, (c + 1) * CHUNK)
            mixed = jnp.dot(ws, vn_ref[rows, cols], preferred_element_type=F32) + bias
            gated_ref[rows, cols] = (u[rows, :] * mixed).astype(BF16)

    o_ref[...] = x + jnp.dot(gated_ref[...], wout_ref[...], preferred_element_type=F32)


def _gmlp_layer(x2d, norm_g, w_in, ln_g, ln_b, w_s, b_s_t, w_out):
    t = x2d.shape[0]
    tm = TM_GMLP
    row = lambda i: (i, 0)
    return pl.pallas_call(
        _gmlp_kernel,
        grid=(t // tm,),
        in_specs=[
            pl.BlockSpec((tm, D_MODEL), row),
            _resident((1, D_MODEL)),
            _resident((D_MODEL, 2 * GMLP_WIDTH)),
            _resident((1, GMLP_WIDTH)),
            _resident((1, GMLP_WIDTH)),
            _resident((GMLP_GROUPS, CHUNK, CHUNK)),
            _resident((CHUNK, GMLP_GROUPS)),
            _resident((GMLP_WIDTH, D_MODEL)),
        ],
        out_specs=pl.BlockSpec((tm, D_MODEL), row),
        out_shape=jax.ShapeDtypeStruct((t, D_MODEL), F32),
        scratch_shapes=[pltpu.VMEM((tm, GMLP_WIDTH), BF16), pltpu.VMEM((tm, GMLP_WIDTH), BF16)],
        compiler_params=_params(),
        name="gmlp",
    )(x2d, norm_g, w_in, ln_g, ln_b, w_s, b_s_t, w_out)


def _ffn_kernel(*refs, has_oproj, has_final_norm):
    refs = list(refs)
    h_ref = refs.pop(0)
    if has_oproj:
        a_ref = refs.pop(0)
        wo_ref = refs.pop(0)
    g_ref, wgu_ref, wd_ref = refs.pop(0), refs.pop(0), refs.pop(0)
    if has_final_norm:
        fg_ref = refs.pop(0)
    (o_ref,) = refs

    h = h_ref[...]
    if has_oproj:
        h = h + jnp.dot(a_ref[...], wo_ref[...], preferred_element_type=F32)
    xn = (_rms_scale(h) * g_ref[...]).astype(BF16)
    gate = jnp.dot(xn, wgu_ref[:, :D_FF], preferred_element_type=F32)
    up = jnp.dot(xn, wgu_ref[:, D_FF:], preferred_element_type=F32)
    act = (jax.nn.silu(gate) * up).astype(BF16)
    h = h + jnp.dot(act, wd_ref[...], preferred_element_type=F32)
    if has_final_norm:
        h = _rms_scale(h) * fg_ref[...]
    o_ref[...] = h


def _ffn_layer(h2d, norm_g, w_gu, w_down, attn=None, w_o=None, final_g=None):
    t = h2d.shape[0]
    tm = TM_FFN
    row = lambda i: (i, 0)
    has_oproj = attn is not None
    has_final_norm = final_g is not None
    args = [h2d]
    specs = [pl.BlockSpec((tm, D_MODEL), row)]
    if has_oproj:
        args += [attn, w_o]
        specs += [pl.BlockSpec((tm, D_V), row), _resident((D_V, D_MODEL))]
    args += [norm_g, w_gu, w_down]
    specs += [_resident((1, D_MODEL)), _resident((D_MODEL, 2 * D_FF)), _resident((D_FF, D_MODEL))]
    if has_final_norm:
        args.append(final_g)
        specs.append(_resident((1, D_MODEL)))
    return pl.pallas_call(
        functools.partial(_ffn_kernel, has_oproj=has_oproj, has_final_norm=has_final_norm),
        grid=(t // tm,),
        in_specs=specs,
        out_specs=pl.BlockSpec((tm, D_MODEL), row),
        out_shape=jax.ShapeDtypeStruct((t, D_MODEL), F32),
        compiler_params=_params(),
        name="ffn_final" if has_final_norm else "ffn",
    )(*args)


def _rope(x, ra, rb, rc):
    half = ROT_DIM // 2
    outs = []
    for j in range(x.shape[1] // LANES):
        xs = x[:, j * LANES:(j + 1) * LANES]
        fwd = pltpu.roll(xs, LANES - half, axis=1)
        bwd = pltpu.roll(xs, half, axis=1)
        outs.append(xs * ra + fwd * rb + bwd * rc)
    return outs


def _qkv_kernel(h_ref, gq_ref, gkv_ref, wq_ref, wkv_ref, ra_ref, rb_ref, rc_ref,
                q_ref, k_ref, v_ref):
    y = _rms_scale(h_ref[...])
    xq = (y * gq_ref[...]).astype(BF16)
    xkv = (y * gkv_ref[...]).astype(BF16)
    ra, rb, rc = ra_ref[...], rb_ref[...], rc_ref[...]
    q = jnp.dot(xq, wq_ref[...], preferred_element_type=F32)
    scale = DIFF_HEAD_DIM ** -0.5
    for j, qs in enumerate(_rope(q, ra, rb, rc)):
        q_ref[:, j * LANES:(j + 1) * LANES] = (qs * scale).astype(BF16)
    k = jnp.dot(xkv, wkv_ref[:, :D_K], preferred_element_type=F32)
    for j, ks in enumerate(_rope(k, ra, rb, rc)):
        k_ref[:, j * LANES:(j + 1) * LANES] = ks.astype(BF16)
    v = jnp.dot(xkv, wkv_ref[:, D_K:], preferred_element_type=F32)
    v_ref[...] = v.astype(BF16)


def _rope_tables(seq):
    half = ROT_DIM // 2
    inv_freq = jnp.power(ROPE_THETA, -jnp.arange(half, dtype=F32) * 2.0 / ROT_DIM)
    ang = jnp.arange(seq, dtype=jnp.int32).astype(F32)[:, None] * inv_freq[None, :]
    cos, sin = jnp.cos(ang), jnp.sin(ang)
    ones = jnp.ones((seq, DIFF_HEAD_DIM - ROT_DIM), F32)
    zeros_h = jnp.zeros((seq, half), F32)
    zeros_r = jnp.zeros((seq, DIFF_HEAD_DIM - ROT_DIM), F32)
    ra = jnp.concatenate([cos, cos, ones], axis=1)
    rb = jnp.concatenate([-sin, zeros_h, zeros_r], axis=1)
    rc = jnp.concatenate([zeros_h, sin, zeros_r], axis=1)
    rep = LANES // DIFF_HEAD_DIM
    return tuple(jnp.tile(a, (1, rep)) for a in (ra, rb, rc))


def _qkv_proj(h2d, seq, gq, gkv, w_q, w_kv):
    t = h2d.shape[0]
    tm = TM_QKV
    blocks_per_seq = seq // tm
    row = lambda i: (i, 0)
    pos = lambda i: (i % blocks_per_seq, 0)
    ra, rb, rc = _rope_tables(seq)
    out = jax.ShapeDtypeStruct((t, D_MODEL), BF16)
    return pl.pallas_call(
        _qkv_kernel,
        grid=(t // tm,),
        in_specs=[
            pl.BlockSpec((tm, D_MODEL), row),
            _resident((1, D_MODEL)),
            _resident((1, D_MODEL)),
            _resident((D_MODEL, D_K)),
            _resident((D_MODEL, D_K + D_V)),
            pl.BlockSpec((tm, LANES), pos),
            pl.BlockSpec((tm, LANES), pos),
            pl.BlockSpec((tm, LANES), pos),
        ],
        out_specs=[pl.BlockSpec((tm, D_MODEL), row)] * 3,
        out_shape=[out, out, out],
        compiler_params=_params(),
        name="qkv",
    )(h2d, gq, gkv, w_q, w_kv, ra, rb, rc)


def _attn_kernel(q_ref, k_ref, v_ref, lq_ref, lk_ref, subg_ref, o_ref, qq_ref, *, lam_init):
    i = pl.program_id(2)
    tq = q_ref.shape[0]

    q = q_ref[...]
    lane = lax.broadcasted_iota(jnp.int32, q.shape, 1)
    zero = jnp.zeros_like(q)
    qq_ref[:tq, :] = jnp.where(lane < DIFF_HEAD_DIM, q, zero)
    qq_ref[tq:, :] = jnp.where(lane >= DIFF_HEAD_DIM, q, zero)
    qq = qq_ref[...]

    def step(j, carry, masked):
        m, l, acc = carry
        kj = k_ref[pl.ds(pl.multiple_of(j * TK, TK), TK), :]
        vj = v_ref[pl.ds(pl.multiple_of(j * TK, TK), TK), :]
        s = lax.dot_general(qq, kj, (((1,), (1,)), ((), ())), preferred_element_type=F32)
        if masked:
            r = lax.broadcasted_iota(jnp.int32, s.shape, 0)
            r = jnp.where(r >= tq, r - tq, r)
            c = lax.broadcasted_iota(jnp.int32, s.shape, 1)
            s = jnp.where(c <= r, s, -jnp.inf)
        m_new = jnp.maximum(m, jnp.max(s, axis=-1, keepdims=True))
        alpha = jnp.exp(m - m_new)
        p = jnp.exp(s - m_new)
        l = alpha * l + jnp.sum(p, axis=-1, keepdims=True)
        acc = alpha * acc + jnp.dot(p.astype(BF16), vj, preferred_element_type=F32)
        return m_new, l, acc

    init = (jnp.full((2 * tq, 1), -jnp.inf, F32), jnp.zeros((2 * tq, 1), F32),
            jnp.zeros((2 * tq, DIFF_V_DIM), F32))
    carry = lax.fori_loop(0, i, lambda j, c: step(j, c, False), init)
    m, l, acc = step(i, carry, True)

    prod = lq_ref[...] * lk_ref[...]
    e = jnp.exp(jnp.sum(prod, axis=-1, keepdims=True))
    lam = e[0:1, :] - e[1:2, :] + lam_init
    o = acc / l
    o = o[:tq, :] - lam * o[tq:, :]
    o = _rms_scale(o) * subg_ref[...]
    o_ref[...] = (o * (1.0 - lam_init)).astype(BF16)


def _diff_attention(q, k, v, batch, seq, lam_q, lam_k, sub_g, lam_init):
    assert TQ == TK
    nq = seq // TQ
    return pl.pallas_call(
        functools.partial(_attn_kernel, lam_init=lam_init),
        grid=(batch, DIFF_HEADS, nq),
        in_specs=[
            pl.BlockSpec((TQ, LANES), lambda b, h, i: (b * nq + i, h)),
            pl.BlockSpec((seq, LANES), lambda b, h, i: (b, h)),
            pl.BlockSpec((seq, LANES), lambda b, h, i: (b, h)),
            _resident((2, DIFF_HEAD_DIM)),
            _resident((2, DIFF_HEAD_DIM)),
            _resident((1, DIFF_V_DIM)),
        ],
        out_specs=pl.BlockSpec((TQ, LANES), lambda b, h, i: (b * nq + i, h)),
        out_shape=jax.ShapeDtypeStruct((batch * seq, D_V), BF16),
        scratch_shapes=[pltpu.VMEM((2 * TQ, LANES), BF16)],
        compiler_params=_params(),
        name="diff_attn",
    )(q, k, v, lam_q, lam_k, sub_g)


def kernel(x, attn_norm_g, ffn_norm_g, gmlp_w_in, gmlp_ln_g, gmlp_ln_b, gmlp_w_s, gmlp_b_s, gmlp_w_out, kv_norm_g, w_kv, diff_w_q, diff_lambda_q, diff_lambda_k, diff_sub_g, diff_w_o, ffn_w_gu, ffn_w_down, final_norm_g):
    batch, seq, _ = x.shape
    depth = attn_norm_g.shape[0]
    assert depth == 2 and gmlp_w_in.shape[0] == N_A and diff_w_q.shape[0] == depth - N_A
    row = lambda a: a.reshape(1, -1)
    h = x.reshape(batch * seq, D_MODEL)

    h = _gmlp_layer(h, row(attn_norm_g[0]), gmlp_w_in[0].astype(BF16), row(gmlp_ln_g[0]),
                    row(gmlp_ln_b[0]), gmlp_w_s[0], gmlp_b_s[0].T, gmlp_w_out[0].astype(BF16))
    h = _ffn_layer(h, row(ffn_norm_g[0]), ffn_w_gu[0].astype(BF16), ffn_w_down[0].astype(BF16))

    q, k, v = _qkv_proj(h, seq, row(attn_norm_g[1]), row(kv_norm_g), diff_w_q[0].astype(BF16),
                        w_kv.astype(BF16))
    lam_init = 0.8 - 0.6 * math.exp(-0.3 * 1)
    attn = _diff_attention(q, k, v, batch, seq, diff_lambda_q[0], diff_lambda_k[0],
                           row(diff_sub_g[0]), lam_init)
    h = _ffn_layer(h, row(ffn_norm_g[1]), ffn_w_gu[1].astype(BF16), ffn_w_down[1].astype(BF16),
                   attn=attn, w_o=diff_w_o[0].astype(BF16), final_g=row(final_norm_g))
    return h.reshape(batch, seq, D_MODEL)
```

```python
import functools
import math

import jax
import jax.numpy as jnp
from jax import lax
from jax.experimental import pallas as pl
from jax.experimental.pallas import tpu as pltpu

D_MODEL = 1024
CHUNK = 128
GMLP_WIDTH = 2 * D_MODEL
GMLP_GROUPS = 8
GMLP_GROUP_DIM = GMLP_WIDTH // GMLP_GROUPS
DIFF_HEADS = 8
DIFF_HEAD_DIM = D_MODEL // (2 * DIFF_HEADS)
DIFF_V_DIM = 2 * DIFF_HEAD_DIM
D_K = DIFF_HEADS * 2 * DIFF_HEAD_DIM
D_V = DIFF_HEADS * DIFF_V_DIM
ROT_DIM = DIFF_HEAD_DIM // 4
ROPE_THETA = 500000.0
D_FF = -(-(8 * D_MODEL) // (3 * 256)) * 256
EPS = 1e-5
N_A = 1

LANES = 128
VMEM_LIMIT_BYTES = 56 * 2**20

TM_GMLP = 256
TM_FFN = 256
TM_QKV = 512
TQ = 512
TK = 256
TC = 256

F32 = jnp.float32
BF16 = jnp.bfloat16


def _rms_scale(x):
    return x * lax.rsqrt(jnp.mean(x * x, axis=-1, keepdims=True) + EPS)


def _gelu(x):
    return 0.5 * x * (1.0 + lax.erf(x * math.sqrt(0.5)))


def _resident(shape):
    return pl.BlockSpec(shape, lambda *_: (0,) * len(shape), pipeline_mode=pl.Buffered(1))


def _params():
    return pltpu.CompilerParams(vmem_limit_bytes=VMEM_LIMIT_BYTES)


def _gmlp_kernel(x_ref, g_ref, win_ref, lng_ref, lnb_ref, ws_ref, bs_ref, wout_ref, o_ref,
                 vn_ref, gated_ref):
    x = x_ref[...]
    xn = (_rms_scale(x) * g_ref[...]).astype(BF16)

    v = jnp.dot(xn, win_ref[:, GMLP_WIDTH:], preferred_element_type=F32)
    v = _gelu(v)
    mu = jnp.mean(v, axis=-1, keepdims=True)
    vc = v - mu
    vn = vc * lax.rsqrt(jnp.mean(vc * vc, axis=-1, keepdims=True) + EPS)
    vn_ref[...] = (vn * lng_ref[...] + lnb_ref[...]).astype(BF16)

    row = lax.broadcasted_iota(jnp.int32, (CHUNK, CHUNK), 0)
    col = lax.broadcasted_iota(jnp.int32, (CHUNK, CHUNK), 1)
    causal = col <= row
    n_chunks = x.shape[0] // CHUNK
    for g in range(GMLP_GROUPS):
        cols = slice(g * GMLP_GROUP_DIM, (g + 1) * GMLP_GROUP_DIM)
        u = jnp.dot(xn, win_ref[:, cols], preferred_element_type=F32)
        u = _gelu(u)
        ws = jnp.where(causal, ws_ref[g], 0.0).astype(BF16)
        bias = bs_ref[:, g:g + 1]
        for c in range(n_chunks):
            rows = slice(c * CHUNK, (c + 1) * CHUNK)
            mixed = jnp.dot(ws, vn_ref[rows, cols], preferred_element_type=F32) + bias
            gated_ref[rows, cols] = (u[rows, :] * mixed).astype(BF16)

    o_ref[...] = x + jnp.dot(gated_ref[...], wout_ref[...], preferred_element_type=F32)


def _gmlp_layer(x2d, norm_g, w_in, ln_g, ln_b, w_s, b_s_t, w_out):
    t = x2d.shape[0]
    tm = TM_GMLP
    row = lambda i: (i, 0)
    return pl.pallas_call(
        _gmlp_kernel,
        grid=(t // tm,),
        in_specs=[
            pl.BlockSpec((tm, D_MODEL), row),
            _resident((1, D_MODEL)),
            _resident((D_MODEL, 2 * GMLP_WIDTH)),
            _resident((1, GMLP_WIDTH)),
            _resident((1, GMLP_WIDTH)),
            _resident((GMLP_GROUPS, CHUNK, CHUNK)),
            _resident((CHUNK, GMLP_GROUPS)),
            _resident((GMLP_WIDTH, D_MODEL)),
        ],
        out_specs=pl.BlockSpec((tm, D_MODEL), row),
        out_shape=jax.ShapeDtypeStruct((t, D_MODEL), F32),
        scratch_shapes=[pltpu.VMEM((tm, GMLP_WIDTH), BF16), pltpu.VMEM((tm, GMLP_WIDTH), BF16)],
        compiler_params=_params(),
        name="gmlp",
    )(x2d, norm_g, w_in, ln_g, ln_b, w_s, b_s_t, w_out)


def _ffn_kernel(*refs, has_oproj, has_final_norm):
    refs = list(refs)
    h_ref = refs.pop(0)
    if has_oproj:
        a_ref = refs.pop(0)
        wo_ref = refs.pop(0)
    g_ref, wgu_ref, wd_ref = refs.pop(0), refs.pop(0), refs.pop(0)
    if has_final_norm:
        fg_ref = refs.pop(0)
    (o_ref,) = refs

    h = h_ref[...]
    if has_oproj:
        h = h + jnp.dot(a_ref[...], wo_ref[...], preferred_element_type=F32)
    xn = (_rms_scale(h) * g_ref[...]).astype(BF16)
    gate = jnp.dot(xn, wgu_ref[:, :D_FF], preferred_element_type=F32)
    up = jnp.dot(xn, wgu_ref[:, D_FF:], preferred_element_type=F32)
    act = (jax.nn.silu(gate) * up).astype(BF16)
    h = h + jnp.dot(act, wd_ref[...], preferred_element_type=F32)
    if has_final_norm:
        h = _rms_scale(h) * fg_ref[...]
    o_ref[...] = h


def _ffn_layer(h2d, norm_g, w_gu, w_down, attn=None, w_o=None, final_g=None):
    t = h2d.shape[0]
    tm = TM_FFN
    row = lambda i: (i, 0)
    has_oproj = attn is not None
    has_final_norm = final_g is not None
    args = [h2d]
    specs = [pl.BlockSpec((tm, D_MODEL), row)]
    if has_oproj:
        args += [attn, w_o]
        specs += [pl.BlockSpec((tm, D_V), row), _resident((D_V, D_MODEL))]
    args += [norm_g, w_gu, w_down]
    specs += [_resident((1, D_MODEL)), _resident((D_MODEL, 2 * D_FF)), _resident((D_FF, D_MODEL))]
    if has_final_norm:
        args.append(final_g)
        specs.append(_resident((1, D_MODEL)))
    return pl.pallas_call(
        functools.partial(_ffn_kernel, has_oproj=has_oproj, has_final_norm=has_final_norm),
        grid=(t // tm,),
        in_specs=specs,
        out_specs=pl.BlockSpec((tm, D_MODEL), row),
        out_shape=jax.ShapeDtypeStruct((t, D_MODEL), F32),
        compiler_params=_params(),
        name="ffn_final" if has_final_norm else "ffn",
    )(*args)


def _rope(x, ra, rb, rc):
    half = ROT_DIM // 2
    outs = []
    for j in range(x.shape[1] // LANES):
        xs = x[:, j * LANES:(j + 1) * LANES]
        fwd = pltpu.roll(xs, LANES - half, axis=1)
        bwd = pltpu.roll(xs, half, axis=1)
        outs.append(xs * ra + fwd * rb + bwd * rc)
    return outs


def _qkv_kernel(h_ref, gq_ref, gkv_ref, wq_ref, wkv_ref, ra_ref, rb_ref, rc_ref,
                qt_ref, k_ref, vt_ref):
    y = _rms_scale(h_ref[...])
    xq = (y * gq_ref[...]).astype(BF16)
    xkv = (y * gkv_ref[...]).astype(BF16)
    ra, rb, rc = ra_ref[...], rb_ref[...], rc_ref[...]
    q = jnp.dot(xq, wq_ref[...], preferred_element_type=F32)
    scale = DIFF_HEAD_DIM ** -0.5
    for j, qs in enumerate(_rope(q, ra, rb, rc)):
        qt_ref[0, j * LANES:(j + 1) * LANES, :] = (qs * scale).T.astype(BF16)
    k = jnp.dot(xkv, wkv_ref[:, :D_K], preferred_element_type=F32)
    for j, ks in enumerate(_rope(k, ra, rb, rc)):
        k_ref[:, j * LANES:(j + 1) * LANES] = ks.astype(BF16)
    v = jnp.dot(xkv, wkv_ref[:, D_K:], preferred_element_type=F32)
    for t in range(v.shape[0] // TK):
        for j in range(D_V // LANES):
            vt_ref[0, t, j * LANES:(j + 1) * LANES, :] = (
                v[t * TK:(t + 1) * TK, j * LANES:(j + 1) * LANES].T.astype(BF16))


def _rope_tables(seq):
    half = ROT_DIM // 2
    inv_freq = jnp.power(ROPE_THETA, -jnp.arange(half, dtype=F32) * 2.0 / ROT_DIM)
    ang = jnp.arange(seq, dtype=jnp.int32).astype(F32)[:, None] * inv_freq[None, :]
    cos, sin = jnp.cos(ang), jnp.sin(ang)
    ones = jnp.ones((seq, DIFF_HEAD_DIM - ROT_DIM), F32)
    zeros_h = jnp.zeros((seq, half), F32)
    zeros_r = jnp.zeros((seq, DIFF_HEAD_DIM - ROT_DIM), F32)
    ra = jnp.concatenate([cos, cos, ones], axis=1)
    rb = jnp.concatenate([-sin, zeros_h, zeros_r], axis=1)
    rc = jnp.concatenate([zeros_h, sin, zeros_r], axis=1)
    rep = LANES // DIFF_HEAD_DIM
    return tuple(jnp.tile(a, (1, rep)) for a in (ra, rb, rc))


def _qkv_proj(h2d, batch, seq, gq, gkv, w_q, w_kv):
    tm = TM_QKV
    assert tm % TK == 0
    nb = seq // tm
    row = lambda b, i: (b * nb + i, 0)
    pos = lambda b, i: (i, 0)
    col = lambda b, i: (b, 0, i)
    ra, rb, rc = _rope_tables(seq)
    return pl.pallas_call(
        _qkv_kernel,
        grid=(batch, nb),
        in_specs=[
            pl.BlockSpec((tm, D_MODEL), row),
            _resident((1, D_MODEL)),
            _resident((1, D_MODEL)),
            _resident((D_MODEL, D_K)),
            _resident((D_MODEL, D_K + D_V)),
            pl.BlockSpec((tm, LANES), pos),
            pl.BlockSpec((tm, LANES), pos),
            pl.BlockSpec((tm, LANES), pos),
        ],
        out_specs=[pl.BlockSpec((1, D_K, tm), col), pl.BlockSpec((tm, D_K), row),
                   pl.BlockSpec((1, tm // TK, D_V, TK), lambda b, i: (b, i, 0, 0))],
        out_shape=[jax.ShapeDtypeStruct((batch, D_K, seq), BF16),
                   jax.ShapeDtypeStruct((batch * seq, D_K), BF16),
                   jax.ShapeDtypeStruct((batch, seq // TK, D_V, TK), BF16)],
        compiler_params=_params(),
        name="qkv",
    )(h2d, gq, gkv, w_q, w_kv, ra, rb, rc)


def _attn_kernel(qt_ref, k_ref, vt_ref, lq_ref, lk_ref, subg_ref, o_ref, qq_ref, acc_ref, *,
                 lam_init):
    i = pl.program_id(2)
    n_half = TQ // TC
    d = DIFF_HEAD_DIM

    qt = qt_ref[0]
    zero = jnp.zeros((d, TQ), BF16)
    qq_ref[:d, :TQ] = qt[:d, :]
    qq_ref[d:, :TQ] = zero
    qq_ref[:d, TQ:] = zero
    qq_ref[d:, TQ:] = qt[d:, :]
    acc_ref[...] = jnp.zeros(acc_ref.shape, F32)

    key_idx = lax.broadcasted_iota(jnp.int32, (TK, TC), 0)
    qry_idx = lax.broadcasted_iota(jnp.int32, (TK, TC), 1)
    causal = key_idx <= qry_idx

    def kv_block(j, ms, ls, first_chain_row):
        kj = k_ref[pl.ds(pl.multiple_of(j * TK, TK), TK), :]
        vtj = vt_ref[0, j]
        ms, ls = list(ms), list(ls)
        chains = [c for c in range(2 * n_half)
                  if first_chain_row is None or c % n_half >= first_chain_row]
        cols = {c: slice(c * TC, (c + 1) * TC) for c in chains}
        s = {c: jnp.dot(kj, qq_ref[:, cols[c]], preferred_element_type=F32) for c in chains}
        p, alpha = {}, {}
        for c in chains:
            if first_chain_row is not None and c % n_half == first_chain_row:
                s[c] = jnp.where(causal, s[c], -jnp.inf)
            m_new = jnp.maximum(ms[c], jnp.max(s[c], axis=0, keepdims=True))
            alpha[c] = jnp.exp(ms[c] - m_new)
            e = jnp.exp(s[c] - m_new)
            ls[c] = alpha[c] * ls[c] + jnp.sum(e, axis=0, keepdims=True)
            ms[c] = m_new
            p[c] = e.astype(BF16)
        for c in chains:
            pv = jnp.dot(vtj, p[c], preferred_element_type=F32)
            acc_ref[:, cols[c]] = alpha[c] * acc_ref[:, cols[c]] + pv
        return tuple(ms), tuple(ls)

    ms = tuple(jnp.full((1, TC), -jnp.inf, F32) for _ in range(2 * n_half))
    ls = tuple(jnp.zeros((1, TC), F32) for _ in range(2 * n_half))
    blocks_per_q = TQ // TK
    ms, ls = lax.fori_loop(0, i * blocks_per_q,
                           lambda j, c: kv_block(j, c[0], c[1], None), (ms, ls))
    for r in range(blocks_per_q):
        ms, ls = kv_block(i * blocks_per_q + r, ms, ls, r)

    prod = lq_ref[...] * lk_ref[...]
    e = jnp.exp(jnp.sum(prod, axis=-1, keepdims=True))
    lam = e[0:1, :] - e[1:2, :] + lam_init
    for r in range(n_half):
        c1, c2 = r, n_half + r
        o1 = acc_ref[:, c1 * TC:(c1 + 1) * TC] * (1.0 / ls[c1])
        o2 = acc_ref[:, c2 * TC:(c2 + 1) * TC] * (1.0 / ls[c2])
        o = (o1 - lam * o2).T
        o = _rms_scale(o) * subg_ref[...]
        o_ref[r * TC:(r + 1) * TC, :] = (o * (1.0 - lam_init)).astype(BF16)


def _diff_attention(qt, k, vt, batch, seq, lam_q, lam_k, sub_g, lam_init):
    assert TC == TK and TQ % TC == 0
    nq = seq // TQ
    return pl.pallas_call(
        functools.partial(_attn_kernel, lam_init=lam_init),
        grid=(batch, DIFF_HEADS, nq),
        in_specs=[
            pl.BlockSpec((1, LANES, TQ), lambda b, h, i: (b, h, i)),
            pl.BlockSpec((seq, LANES), lambda b, h, i: (b, h)),
            pl.BlockSpec((1, seq // TK, LANES, TK), lambda b, h, i: (b, 0, h, 0)),
            _resident((2, DIFF_HEAD_DIM)),
            _resident((2, DIFF_HEAD_DIM)),
            _resident((1, DIFF_V_DIM)),
        ],
        out_specs=pl.BlockSpec((TQ, LANES), lambda b, h, i: (b * nq + i, h)),
        out_shape=jax.ShapeDtypeStruct((batch * seq, D_V), BF16),
        scratch_shapes=[pltpu.VMEM((LANES, 2 * TQ), BF16), pltpu.VMEM((DIFF_V_DIM, 2 * TQ), F32)],
        compiler_params=_params(),
        name="diff_attn",
    )(qt, k, vt, lam_q, lam_k, sub_g)


def kernel(x, attn_norm_g, ffn_norm_g, gmlp_w_in, gmlp_ln_g, gmlp_ln_b, gmlp_w_s, gmlp_b_s, gmlp_w_out, kv_norm_g, w_kv, diff_w_q, diff_lambda_q, diff_lambda_k, diff_sub_g, diff_w_o, ffn_w_gu, ffn_w_down, final_norm_g):
    batch, seq, _ = x.shape
    depth = attn_norm_g.shape[0]
    assert depth == 2 and gmlp_w_in.shape[0] == N_A and diff_w_q.shape[0] == depth - N_A
    row = lambda a: a.reshape(1, -1)
    h = x.reshape(batch * seq, D_MODEL)

    h = _gmlp_layer(h, row(attn_norm_g[0]), gmlp_w_in[0].astype(BF16), row(gmlp_ln_g[0]),
                    row(gmlp_ln_b[0]), gmlp_w_s[0], gmlp_b_s[0].T, gmlp_w_out[0].astype(BF16))
    h = _ffn_layer(h, row(ffn_norm_g[0]), ffn_w_gu[0].astype(BF16), ffn_w_down[0].astype(BF16))

    qt, k, vt = _qkv_proj(h, batch, seq, row(attn_norm_g[1]), row(kv_norm_g),
                          diff_w_q[0].astype(BF16), w_kv.astype(BF16))
    lam_init = 0.8 - 0.6 * math.exp(-0.3 * 1)
    attn = _diff_attention(qt, k, vt, batch, seq, diff_lambda_q[0], diff_lambda_k[0],
                           row(diff_sub_g[0]), lam_init)
    h = _ffn_layer(h, row(ffn_norm_g[1]), ffn_w_gu[1].astype(BF16), ffn_w_down[1].astype(BF16),
                   attn=attn, w_o=diff_w_o[0].astype(BF16), final_g=row(final_norm_g))
    return h.reshape(batch, seq, D_MODEL)
```

```python
import functools
import math

import jax
import jax.numpy as jnp
from jax import lax
from jax.experimental import pallas as pl
from jax.experimental.pallas import tpu as pltpu

D_MODEL = 1024
CHUNK = 128
GMLP_WIDTH = 2 * D_MODEL
GMLP_GROUPS = 8
GMLP_GROUP_DIM = GMLP_WIDTH // GMLP_GROUPS
DIFF_HEADS = 8
DIFF_HEAD_DIM = D_MODEL // (2 * DIFF_HEADS)
DIFF_V_DIM = 2 * DIFF_HEAD_DIM
D_K = DIFF_HEADS * 2 * DIFF_HEAD_DIM
D_V = DIFF_HEADS * DIFF_V_DIM
ROT_DIM = DIFF_HEAD_DIM // 4
ROPE_THETA = 500000.0
D_FF = -(-(8 * D_MODEL) // (3 * 256)) * 256
EPS = 1e-5
N_A = 1

LANES = 128
VMEM_LIMIT_BYTES = 56 * 2**20

TM_GMLP = 256
TM_FFN = 256
TM_QKV = 512
TQ = 512
TK = 256
TC = 256

F32 = jnp.float32
BF16 = jnp.bfloat16


def _rms_scale(x):
    return x * lax.rsqrt(jnp.mean(x * x, axis=-1, keepdims=True) + EPS)


def _gelu(x):
    return 0.5 * x * (1.0 + lax.erf(x * math.sqrt(0.5)))


def _resident(shape):
    return pl.BlockSpec(shape, lambda *_: (0,) * len(shape), pipeline_mode=pl.Buffered(1))


def _params():
    return pltpu.CompilerParams(vmem_limit_bytes=VMEM_LIMIT_BYTES)


def _gmlp_kernel(x_ref, g_ref, win_ref, lng_ref, lnb_ref, ws_ref, bs_ref, wout_ref, o_ref,
                 vn_ref, gated_ref):
    x = x_ref[...]
    xn = (_rms_scale(x) * g_ref[...]).astype(BF16)

    v = jnp.dot(xn, win_ref[:, GMLP_WIDTH:], preferred_element_type=F32)
    v = _gelu(v)
    mu = jnp.mean(v, axis=-1, keepdims=True)
    vc = v - mu
    vn = vc * lax.rsqrt(jnp.mean(vc * vc, axis=-1, keepdims=True) + EPS)
    vn_ref[...] = (vn * lng_ref[...] + lnb_ref[...]).astype(BF16)

    row = lax.broadcasted_iota(jnp.int32, (CHUNK, CHUNK), 0)
    col = lax.broadcasted_iota(jnp.int32, (CHUNK, CHUNK), 1)
    causal = col <= row
    n_chunks = x.shape[0] // CHUNK
    for g in range(GMLP_GROUPS):
        cols = slice(g * GMLP_GROUP_DIM, (g + 1) * GMLP_GROUP_DIM)
        u = jnp.dot(xn, win_ref[:, cols], preferred_element_type=F32)
        u = _gelu(u)
        ws = jnp.where(causal, ws_ref[g], 0.0).astype(BF16)
        bias = bs_ref[:, g:g + 1]
        for c in range(n_chunks):
            rows = slice(c * CHUNK, (c + 1) * CHUNK)
            mixed = jnp.dot(ws, vn_ref[rows, cols], preferred_element_type=F32) + bias
            gated_ref[rows, cols] = (u[rows, :] * mixed).astype(BF16)

    o_ref[...] = x + jnp.dot(gated_ref[...], wout_ref[...], preferred_element_type=F32)


def _gmlp_layer(x2d, norm_g, w_in, ln_g, ln_b, w_s, b_s_t, w_out):
    t = x2d.shape[0]
    tm = TM_GMLP
    row = lambda i: (i, 0)
    return pl.pallas_call(
        _gmlp_kernel,
        grid=(t // tm,),
        in_specs=[
            pl.BlockSpec((tm, D_MODEL), row),
            _resident((1, D_MODEL)),
            _resident((D_MODEL, 2 * GMLP_WIDTH)),
            _resident((1, GMLP_WIDTH)),
            _resident((1, GMLP_WIDTH)),
            _resident((GMLP_GROUPS, CHUNK, CHUNK)),
            _resident((CHUNK, GMLP_GROUPS)),
            _resident((GMLP_WIDTH, D_MODEL)),
        ],
        out_specs=pl.BlockSpec((tm, D_MODEL), row),
        out_shape=jax.ShapeDtypeStruct((t, D_MODEL), F32),
        scratch_shapes=[pltpu.VMEM((tm, GMLP_WIDTH), BF16), pltpu.VMEM((tm, GMLP_WIDTH), BF16)],
        compiler_params=_params(),
        name="gmlp",
    )(x2d, norm_g, w_in, ln_g, ln_b, w_s, b_s_t, w_out)


def _ffn_kernel(*refs, has_oproj, has_final_norm):
    refs = list(refs)
    h_ref = refs.pop(0)
    if has_oproj:
        a_ref = refs.pop(0)
        wo_ref = refs.pop(0)
    g_ref, wgu_ref, wd_ref = refs.pop(0), refs.pop(0), refs.pop(0)
    if has_final_norm:
        fg_ref = refs.pop(0)
    (o_ref,) = refs

    h = h_ref[...]
    if has_oproj:
        h = h + jnp.dot(a_ref[...], wo_ref[...], preferred_element_type=F32)
    xn = (_rms_scale(h) * g_ref[...]).astype(BF16)
    gate = jnp.dot(xn, wgu_ref[:, :D_FF], preferred_element_type=F32)
    up = jnp.dot(xn, wgu_ref[:, D_FF:], preferred_element_type=F32)
    act = (jax.nn.silu(gate) * up).astype(BF16)
    h = h + jnp.dot(act, wd_ref[...], preferred_element_type=F32)
    if has_final_norm:
        h = _rms_scale(h) * fg_ref[...]
    o_ref[...] = h


def _ffn_layer(h2d, norm_g, w_gu, w_down, attn=None, w_o=None, final_g=None):
    t = h2d.shape[0]
    tm = TM_FFN
    row = lambda i: (i, 0)
    has_oproj = attn is not None
    has_final_norm = final_g is not None
    args = [h2d]
    specs = [pl.BlockSpec((tm, D_MODEL), row)]
    if has_oproj:
        args += [attn, w_o]
        specs += [pl.BlockSpec((tm, D_V), row), _resident((D_V, D_MODEL))]
    args += [norm_g, w_gu, w_down]
    specs += [_resident((1, D_MODEL)), _resident((D_MODEL, 2 * D_FF)), _resident((D_FF, D_MODEL))]
    if has_final_norm:
        args.append(final_g)
        specs.append(_resident((1, D_MODEL)))
    return pl.pallas_call(
        functools.partial(_ffn_kernel, has_oproj=has_oproj, has_final_norm=has_final_norm),
        grid=(t // tm,),
        in_specs=specs,
        out_specs=pl.BlockSpec((tm, D_MODEL), row),
        out_shape=jax.ShapeDtypeStruct((t, D_MODEL), F32),
        compiler_params=_params(),
        name="ffn_final" if has_final_norm else "ffn",
    )(*args)


def _rope(x, ra, rb, rc):
    half = ROT_DIM // 2
    outs = []
    for j in range(x.shape[1] // LANES):
        xs = x[:, j * LANES:(j + 1) * LANES]
        fwd = pltpu.roll(xs, LANES - half, axis=1)
        bwd = pltpu.roll(xs, half, axis=1)
        outs.append(xs * ra + fwd * rb + bwd * rc)
    return outs


def _qkv_kernel(h_ref, gq_ref, gkv_ref, wq_ref, wkv_ref, ra_ref, rb_ref, rc_ref,
                qt_ref, k_ref, vt_ref):
    y = _rms_scale(h_ref[...])
    xq = (y * gq_ref[...]).astype(BF16)
    xkv = (y * gkv_ref[...]).astype(BF16)
    ra, rb, rc = ra_ref[...], rb_ref[...], rc_ref[...]
    q = jnp.dot(xq, wq_ref[...], preferred_element_type=F32)
    scale = DIFF_HEAD_DIM ** -0.5
    for j, qs in enumerate(_rope(q, ra, rb, rc)):
        qt_ref[0, j * LANES:(j + 1) * LANES, :] = (qs * scale).T.astype(BF16)
    k = jnp.dot(xkv, wkv_ref[:, :D_K], preferred_element_type=F32)
    for j, ks in enumerate(_rope(k, ra, rb, rc)):
        k_ref[:, j * LANES:(j + 1) * LANES] = ks.astype(BF16)
    v = jnp.dot(xkv, wkv_ref[:, D_K:], preferred_element_type=F32)
    for t in range(v.shape[0] // TK):
        for j in range(D_V // LANES):
            vt_ref[0, t, j * LANES:(j + 1) * LANES, :] = (
                v[t * TK:(t + 1) * TK, j * LANES:(j + 1) * LANES].T.astype(BF16))


def _rope_tables(seq):
    half = ROT_DIM // 2
    inv_freq = jnp.power(ROPE_THETA, -jnp.arange(half, dtype=F32) * 2.0 / ROT_DIM)
    ang = jnp.arange(seq, dtype=jnp.int32).astype(F32)[:, None] * inv_freq[None, :]
    cos, sin = jnp.cos(ang), jnp.sin(ang)
    ones = jnp.ones((seq, DIFF_HEAD_DIM - ROT_DIM), F32)
    zeros_h = jnp.zeros((seq, half), F32)
    zeros_r = jnp.zeros((seq, DIFF_HEAD_DIM - ROT_DIM), F32)
    ra = jnp.concatenate([cos, cos, ones], axis=1)
    rb = jnp.concatenate([-sin, zeros_h, zeros_r], axis=1)
    rc = jnp.concatenate([zeros_h, sin, zeros_r], axis=1)
    rep = LANES // DIFF_HEAD_DIM
    return tuple(jnp.tile(a, (1, rep)) for a in (ra, rb, rc))


def _qkv_proj(h2d, batch, seq, gq, gkv, w_q, w_kv):
    tm = TM_QKV
    assert tm % TK == 0
    nb = seq // tm
    row = lambda b, i: (b * nb + i, 0)
    pos = lambda b, i: (i, 0)
    col = lambda b, i: (b, 0, i)
    ra, rb, rc = _rope_tables(seq)
    return pl.pallas_call(
        _qkv_kernel,
        grid=(batch, nb),
        in_specs=[
            pl.BlockSpec((tm, D_MODEL), row),
            _resident((1, D_MODEL)),
            _resident((1, D_MODEL)),
            _resident((D_MODEL, D_K)),
            _resident((D_MODEL, D_K + D_V)),
            pl.BlockSpec((tm, LANES), pos),
            pl.BlockSpec((tm, LANES), pos),
            pl.BlockSpec((tm, LANES), pos),
        ],
        out_specs=[pl.BlockSpec((1, D_K, tm), col), pl.BlockSpec((tm, D_K), row),
                   pl.BlockSpec((1, tm // TK, D_V, TK), lambda b, i: (b, i, 0, 0))],
        out_shape=[jax.ShapeDtypeStruct((batch, D_K, seq), BF16),
                   jax.ShapeDtypeStruct((batch * seq, D_K), BF16),
                   jax.ShapeDtypeStruct((batch, seq // TK, D_V, TK), BF16)],
        compiler_params=_params(),
        name="qkv",
    )(h2d, gq, gkv, w_q, w_kv, ra, rb, rc)


def _attn_kernel(qt_ref, k_ref, vt_ref, lq_ref, lk_ref, subg_ref, o_ref, qq_ref, acc_ref, s_ref, *,
                 lam_init):
    i = pl.program_id(2)
    n_half = TQ // TC
    d = DIFF_HEAD_DIM

    qt = qt_ref[0]
    zero = jnp.zeros((d, TQ), BF16)
    qq_ref[:d, :TQ] = qt[:d, :]
    qq_ref[d:, :TQ] = zero
    qq_ref[:d, TQ:] = zero
    qq_ref[d:, TQ:] = qt[d:, :]
    acc_ref[...] = jnp.zeros(acc_ref.shape, F32)

    key_idx = lax.broadcasted_iota(jnp.int32, (TK, TC), 0)
    qry_idx = lax.broadcasted_iota(jnp.int32, (TK, TC), 1)
    causal = key_idx <= qry_idx

    all_chains = list(range(2 * n_half))
    cols = {c: slice(c * TC, (c + 1) * TC) for c in all_chains}

    def chains_from(first_chain_row):
        return [c for c in all_chains if first_chain_row is None or c % n_half >= first_chain_row]

    def scores(j, slot, first_chain_row):
        kj = k_ref[pl.ds(pl.multiple_of(j * TK, TK), TK), :]
        for c in chains_from(first_chain_row):
            s_ref[slot, :, cols[c]] = jnp.dot(kj, qq_ref[:, cols[c]], preferred_element_type=F32)

    def softmax_pv(j, slot, ms, ls, first_chain_row):
        vtj = vt_ref[0, j]
        ms, ls = list(ms), list(ls)
        chains = chains_from(first_chain_row)
        p, alpha = {}, {}
        for c in chains:
            s = s_ref[slot, :, cols[c]]
            if first_chain_row is not None and c % n_half == first_chain_row:
                s = jnp.where(causal, s, -jnp.inf)
            m_new = jnp.maximum(ms[c], jnp.max(s, axis=0, keepdims=True))
            alpha[c] = jnp.exp(ms[c] - m_new)
            e = jnp.exp(s - m_new)
            ls[c] = alpha[c] * ls[c] + jnp.sum(e, axis=0, keepdims=True)
            ms[c] = m_new
            p[c] = e.astype(BF16)
        for c in chains:
            pv = jnp.dot(vtj, p[c], preferred_element_type=F32)
            acc_ref[:, cols[c]] = alpha[c] * acc_ref[:, cols[c]] + pv
        return tuple(ms), tuple(ls)

    def pair(t, carry):
        ms, ls = carry
        scores(2 * t + 1, 1, None)
        ms, ls = softmax_pv(2 * t, 0, ms, ls, None)
        scores(2 * t + 2, 0, None)
        return softmax_pv(2 * t + 1, 1, ms, ls, None)

    ms = tuple(jnp.full((1, TC), -jnp.inf, F32) for _ in all_chains)
    ls = tuple(jnp.zeros((1, TC), F32) for _ in all_chains)
    scores(0, 0, None)
    ms, ls = lax.fori_loop(0, i, pair, (ms, ls))
    scores(2 * i + 1, 1, 1)
    ms, ls = softmax_pv(2 * i, 0, ms, ls, 0)
    ms, ls = softmax_pv(2 * i + 1, 1, ms, ls, 1)

    prod = lq_ref[...] * lk_ref[...]
    e = jnp.exp(jnp.sum(prod, axis=-1, keepdims=True))
    lam = e[0:1, :] - e[1:2, :] + lam_init
    for r in range(n_half):
        c1, c2 = r, n_half + r
        o1 = acc_ref[:, c1 * TC:(c1 + 1) * TC] * (1.0 / ls[c1])
        o2 = acc_ref[:, c2 * TC:(c2 + 1) * TC] * (1.0 / ls[c2])
        o = (o1 - lam * o2).T
        o = _rms_scale(o) * subg_ref[...]
        o_ref[r * TC:(r + 1) * TC, :] = (o * (1.0 - lam_init)).astype(BF16)


def _diff_attention(qt, k, vt, batch, seq, lam_q, lam_k, sub_g, lam_init):
    assert TC == TK and TQ == 2 * TK
    nq = seq // TQ
    return pl.pallas_call(
        functools.partial(_attn_kernel, lam_init=lam_init),
        grid=(batch, DIFF_HEADS, nq),
        in_specs=[
            pl.BlockSpec((1, LANES, TQ), lambda b, h, i: (b, h, i)),
            pl.BlockSpec((seq, LANES), lambda b, h, i: (b, h)),
            pl.BlockSpec((1, seq // TK, LANES, TK), lambda b, h, i: (b, 0, h, 0)),
            _resident((2, DIFF_HEAD_DIM)),
            _resident((2, DIFF_HEAD_DIM)),
            _resident((1, DIFF_V_DIM)),
        ],
        out_specs=pl.BlockSpec((TQ, LANES), lambda b, h, i: (b * nq + i, h)),
        out_shape=jax.ShapeDtypeStruct((batch * seq, D_V), BF16),
        scratch_shapes=[pltpu.VMEM((LANES, 2 * TQ), BF16), pltpu.VMEM((DIFF_V_DIM, 2 * TQ), F32),
                        pltpu.VMEM((2, TK, 2 * TQ), F32)],
        compiler_params=_params(),
        name="diff_attn",
    )(qt, k, vt, lam_q, lam_k, sub_g)


def kernel(x, attn_norm_g, ffn_norm_g, gmlp_w_in, gmlp_ln_g, gmlp_ln_b, gmlp_w_s, gmlp_b_s, gmlp_w_out, kv_norm_g, w_kv, diff_w_q, diff_lambda_q, diff_lambda_k, diff_sub_g, diff_w_o, ffn_w_gu, ffn_w_down, final_norm_g):
    batch, seq, _ = x.shape
    depth = attn_norm_g.shape[0]
    assert depth == 2 and gmlp_w_in.shape[0] == N_A and diff_w_q.shape[0] == depth - N_A
    row = lambda a: a.reshape(1, -1)
    h = x.reshape(batch * seq, D_MODEL)

    h = _gmlp_layer(h, row(attn_norm_g[0]), gmlp_w_in[0].astype(BF16), row(gmlp_ln_g[0]),
                    row(gmlp_ln_b[0]), gmlp_w_s[0], gmlp_b_s[0].T, gmlp_w_out[0].astype(BF16))
    h = _ffn_layer(h, row(ffn_norm_g[0]), ffn_w_gu[0].astype(BF16), ffn_w_down[0].astype(BF16))

    qt, k, vt = _qkv_proj(h, batch, seq, row(attn_norm_g[1]), row(kv_norm_g),
                          diff_w_q[0].astype(BF16), w_kv.astype(BF16))
    lam_init = 0.8 - 0.6 * math.exp(-0.3 * 1)
    attn = _diff_attention(qt, k, vt, batch, seq, diff_lambda_q[0], diff_lambda_k[0],
                           row(diff_sub_g[0]), lam_init)
    h = _ffn_layer(h, row(ffn_norm_g[1]), ffn_w_gu[1].astype(BF16), ffn_w_down[1].astype(BF16),
                   attn=attn, w_o=diff_w_o[0].astype(BF16), final_g=row(final_norm_g))
    return h.reshape(batch, seq, D_MODEL)
```

```python
import functools
import math

import jax
import jax.numpy as jnp
from jax import lax
from jax.experimental import pallas as pl
from jax.experimental.pallas import tpu as pltpu

D_MODEL = 1024
CHUNK = 128
GMLP_WIDTH = 2 * D_MODEL
GMLP_GROUPS = 8
GMLP_GROUP_DIM = GMLP_WIDTH // GMLP_GROUPS
DIFF_HEADS = 8
DIFF_HEAD_DIM = D_MODEL // (2 * DIFF_HEADS)
DIFF_V_DIM = 2 * DIFF_HEAD_DIM
D_K = DIFF_HEADS * 2 * DIFF_HEAD_DIM
D_V = DIFF_HEADS * DIFF_V_DIM
ROT_DIM = DIFF_HEAD_DIM // 4
ROPE_THETA = 500000.0
D_FF = -(-(8 * D_MODEL) // (3 * 256)) * 256
EPS = 1e-5
N_A = 1

LANES = 128
VMEM_LIMIT_BYTES = 56 * 2**20

TM_GMLP = 256
TM_FFN = 256
TM_QKV = 512
TQ = 1024
TK = 256
TC = 256
BF16_SUBLANES = 16
VT_ROWS = DIFF_V_DIM + BF16_SUBLANES

F32 = jnp.float32
BF16 = jnp.bfloat16


def _rms_scale(x):
    return x * lax.rsqrt(jnp.mean(x * x, axis=-1, keepdims=True) + EPS)


def _gelu(x):
    return 0.5 * x * (1.0 + lax.erf(x * math.sqrt(0.5)))


def _resident(shape):
    return pl.BlockSpec(shape, lambda *_: (0,) * len(shape), pipeline_mode=pl.Buffered(1))


def _params():
    return pltpu.CompilerParams(vmem_limit_bytes=VMEM_LIMIT_BYTES)


def _gmlp_kernel(x_ref, g_ref, win_ref, lng_ref, lnb_ref, ws_ref, bs_ref, wout_ref, o_ref,
                 vn_ref, gated_ref):
    x = x_ref[...]
    xn = (_rms_scale(x) * g_ref[...]).astype(BF16)

    v = jnp.dot(xn, win_ref[:, GMLP_WIDTH:], preferred_element_type=F32)
    v = _gelu(v)
    mu = jnp.mean(v, axis=-1, keepdims=True)
    vc = v - mu
    vn = vc * lax.rsqrt(jnp.mean(vc * vc, axis=-1, keepdims=True) + EPS)
    vn_ref[...] = (vn * lng_ref[...] + lnb_ref[...]).astype(BF16)

    row = lax.broadcasted_iota(jnp.int32, (CHUNK, CHUNK), 0)
    col = lax.broadcasted_iota(jnp.int32, (CHUNK, CHUNK), 1)
    causal = col <= row
    n_chunks = x.shape[0] // CHUNK
    for g in range(GMLP_GROUPS):
        cols = slice(g * GMLP_GROUP_DIM, (g + 1) * GMLP_GROUP_DIM)
        u = jnp.dot(xn, win_ref[:, cols], preferred_element_type=F32)
        u = _gelu(u)
        ws = jnp.where(causal, ws_ref[g], 0.0).astype(BF16)
        bias = bs_ref[:, g:g + 1]
        for c in range(n_chunks):
            rows = slice(c * CHUNK, (c + 1) * CHUNK)
            mixed = jnp.dot(ws, vn_ref[rows, cols], preferred_element_type=F32) + bias
            gated_ref[rows, cols] = (u[rows, :] * mixed).astype(BF16)

    o_ref[...] = x + jnp.dot(gated_ref[...], wout_ref[...], preferred_element_type=F32)


def _gmlp_layer(x2d, norm_g, w_in, ln_g, ln_b, w_s, b_s_t, w_out):
    t = x2d.shape[0]
    tm = TM_GMLP
    row = lambda i: (i, 0)
    return pl.pallas_call(
        _gmlp_kernel,
        grid=(t // tm,),
        in_specs=[
            pl.BlockSpec((tm, D_MODEL), row),
            _resident((1, D_MODEL)),
            _resident((D_MODEL, 2 * GMLP_WIDTH)),
            _resident((1, GMLP_WIDTH)),
            _resident((1, GMLP_WIDTH)),
            _resident((GMLP_GROUPS, CHUNK, CHUNK)),
            _resident((CHUNK, GMLP_GROUPS)),
            _resident((GMLP_WIDTH, D_MODEL)),
        ],
        out_specs=pl.BlockSpec((tm, D_MODEL), row),
        out_shape=jax.ShapeDtypeStruct((t, D_MODEL), F32),
        scratch_shapes=[pltpu.VMEM((tm, GMLP_WIDTH), BF16), pltpu.VMEM((tm, GMLP_WIDTH), BF16)],
        compiler_params=_params(),
        name="gmlp",
    )(x2d, norm_g, w_in, ln_g, ln_b, w_s, b_s_t, w_out)


def _ffn_kernel(*refs, has_oproj, has_final_norm):
    refs = list(refs)
    h_ref = refs.pop(0)
    if has_oproj:
        a_ref = refs.pop(0)
        wo_ref = refs.pop(0)
    g_ref, wgu_ref, wd_ref = refs.pop(0), refs.pop(0), refs.pop(0)
    if has_final_norm:
        fg_ref = refs.pop(0)
    (o_ref,) = refs

    h = h_ref[...]
    if has_oproj:
        h = h + jnp.dot(a_ref[...], wo_ref[...], preferred_element_type=F32)
    xn = (_rms_scale(h) * g_ref[...]).astype(BF16)
    gate = jnp.dot(xn, wgu_ref[:, :D_FF], preferred_element_type=F32)
    up = jnp.dot(xn, wgu_ref[:, D_FF:], preferred_element_type=F32)
    act = (jax.nn.silu(gate) * up).astype(BF16)
    h = h + jnp.dot(act, wd_ref[...], preferred_element_type=F32)
    if has_final_norm:
        h = _rms_scale(h) * fg_ref[...]
    o_ref[...] = h


def _ffn_layer(h2d, norm_g, w_gu, w_down, attn=None, w_o=None, final_g=None):
    t = h2d.shape[0]
    tm = TM_FFN
    row = lambda i: (i, 0)
    has_oproj = attn is not None
    has_final_norm = final_g is not None
    args = [h2d]
    specs = [pl.BlockSpec((tm, D_MODEL), row)]
    if has_oproj:
        args += [attn, w_o]
        specs += [pl.BlockSpec((tm, D_V), row), _resident((D_V, D_MODEL))]
    args += [norm_g, w_gu, w_down]
    specs += [_resident((1, D_MODEL)), _resident((D_MODEL, 2 * D_FF)), _resident((D_FF, D_MODEL))]
    if has_final_norm:
        args.append(final_g)
        specs.append(_resident((1, D_MODEL)))
    return pl.pallas_call(
        functools.partial(_ffn_kernel, has_oproj=has_oproj, has_final_norm=has_final_norm),
        grid=(t // tm,),
        in_specs=specs,
        out_specs=pl.BlockSpec((tm, D_MODEL), row),
        out_shape=jax.ShapeDtypeStruct((t, D_MODEL), F32),
        compiler_params=_params(),
        name="ffn_final" if has_final_norm else "ffn",
    )(*args)


def _rope(x, ra, rb, rc):
    half = ROT_DIM // 2
    outs = []
    for j in range(x.shape[1] // LANES):
        xs = x[:, j * LANES:(j + 1) * LANES]
        fwd = pltpu.roll(xs, LANES - half, axis=1)
        bwd = pltpu.roll(xs, half, axis=1)
        outs.append(xs * ra + fwd * rb + bwd * rc)
    return outs


def _qkv_kernel(h_ref, gq_ref, gkv_ref, wq_ref, wkv_ref, ra_ref, rb_ref, rc_ref,
                qt_ref, k_ref, vt_ref):
    y = _rms_scale(h_ref[...])
    xq = (y * gq_ref[...]).astype(BF16)
    xkv = (y * gkv_ref[...]).astype(BF16)
    ra, rb, rc = ra_ref[...], rb_ref[...], rc_ref[...]
    q = jnp.dot(xq, wq_ref[...], preferred_element_type=F32)
    scale = DIFF_HEAD_DIM ** -0.5 * math.log2(math.e)
    for j, qs in enumerate(_rope(q, ra, rb, rc)):
        qt_ref[0, j * LANES:(j + 1) * LANES, :] = (qs * scale).T.astype(BF16)
    k = jnp.dot(xkv, wkv_ref[:, :D_K], preferred_element_type=F32)
    for j, ks in enumerate(_rope(k, ra, rb, rc)):
        k_ref[:, j * LANES:(j + 1) * LANES] = ks.astype(BF16)
    v = jnp.dot(xkv, wkv_ref[:, D_K:], preferred_element_type=F32)
    ones = jnp.ones((VT_ROWS - DIFF_V_DIM, TK), BF16)
    for t in range(v.shape[0] // TK):
        for j in range(DIFF_HEADS):
            vt_ref[0, t, j * VT_ROWS:j * VT_ROWS + DIFF_V_DIM, :] = (
                v[t * TK:(t + 1) * TK, j * DIFF_V_DIM:(j + 1) * DIFF_V_DIM].T.astype(BF16))
            vt_ref[0, t, j * VT_ROWS + DIFF_V_DIM:(j + 1) * VT_ROWS, :] = ones


def _rope_tables(seq):
    half = ROT_DIM // 2
    inv_freq = jnp.power(ROPE_THETA, -jnp.arange(half, dtype=F32) * 2.0 / ROT_DIM)
    ang = jnp.arange(seq, dtype=jnp.int32).astype(F32)[:, None] * inv_freq[None, :]
    cos, sin = jnp.cos(ang), jnp.sin(ang)
    ones = jnp.ones((seq, DIFF_HEAD_DIM - ROT_DIM), F32)
    zeros_h = jnp.zeros((seq, half), F32)
    zeros_r = jnp.zeros((seq, DIFF_HEAD_DIM - ROT_DIM), F32)
    ra = jnp.concatenate([cos, cos, ones], axis=1)
    rb = jnp.concatenate([-sin, zeros_h, zeros_r], axis=1)
    rc = jnp.concatenate([zeros_h, sin, zeros_r], axis=1)
    rep = LANES // DIFF_HEAD_DIM
    return tuple(jnp.tile(a, (1, rep)) for a in (ra, rb, rc))


def _qkv_proj(h2d, batch, seq, gq, gkv, w_q, w_kv):
    tm = TM_QKV
    assert tm % TK == 0
    nb = seq // tm
    row = lambda b, i: (b * nb + i, 0)
    pos = lambda b, i: (i, 0)
    col = lambda b, i: (b, 0, i)
    ra, rb, rc = _rope_tables(seq)
    return pl.pallas_call(
        _qkv_kernel,
        grid=(batch, nb),
        in_specs=[
            pl.BlockSpec((tm, D_MODEL), row),
            _resident((1, D_MODEL)),
            _resident((1, D_MODEL)),
            _resident((D_MODEL, D_K)),
            _resident((D_MODEL, D_K + D_V)),
            pl.BlockSpec((tm, LANES), pos),
            pl.BlockSpec((tm, LANES), pos),
            pl.BlockSpec((tm, LANES), pos),
        ],
        out_specs=[pl.BlockSpec((1, D_K, tm), col), pl.BlockSpec((tm, D_K), row),
                   pl.BlockSpec((1, tm // TK, DIFF_HEADS * VT_ROWS, TK), lambda b, i: (b, i, 0, 0))],
        out_shape=[jax.ShapeDtypeStruct((batch, D_K, seq), BF16),
                   jax.ShapeDtypeStruct((batch * seq, D_K), BF16),
                   jax.ShapeDtypeStruct((batch, seq // TK, DIFF_HEADS * VT_ROWS, TK), BF16)],
        compiler_params=_params(),
        name="qkv",
    )(h2d, gq, gkv, w_q, w_kv, ra, rb, rc)


def _attn_kernel(qt_ref, k_ref, vt_ref, lq_ref, lk_ref, subg_ref, o_ref, qq_ref, acc_ref, s_ref, *,
                 lam_init):
    i = pl.program_id(2)
    n_half = TQ // TC
    d = DIFF_HEAD_DIM
    dv = DIFF_V_DIM

    qt = qt_ref[0]
    zero = jnp.zeros((d, TQ), BF16)
    qq_ref[:d, :TQ] = qt[:d, :]
    qq_ref[d:, :TQ] = zero
    qq_ref[:d, TQ:] = zero
    qq_ref[d:, TQ:] = qt[d:, :]
    acc_ref[...] = jnp.zeros(acc_ref.shape, F32)

    key_idx = lax.broadcasted_iota(jnp.int32, (TK, TC), 0)
    qry_idx = lax.broadcasted_iota(jnp.int32, (TK, TC), 1)
    causal = key_idx <= qry_idx

    all_chains = list(range(2 * n_half))
    cols = {c: slice(c * TC, (c + 1) * TC) for c in all_chains}

    def chains_from(first_chain_row):
        return [c for c in all_chains if first_chain_row is None or c % n_half >= first_chain_row]

    def scores(j, slot, first_chain_row):
        kj = k_ref[pl.ds(pl.multiple_of(j * TK, TK), TK), :]
        for c in chains_from(first_chain_row):
            s_ref[slot, :, cols[c]] = jnp.dot(kj, qq_ref[:, cols[c]], preferred_element_type=F32)

    def softmax_pv(j, slot, ms, first_chain_row):
        vtj = vt_ref[0, j]
        ms = list(ms)
        chains = chains_from(first_chain_row)
        p, alpha = {}, {}
        for c in chains:
            s = s_ref[slot, :, cols[c]]
            if first_chain_row is not None and c % n_half == first_chain_row:
                s = jnp.where(causal, s, -jnp.inf)
            m_new = jnp.maximum(ms[c], jnp.max(s, axis=0, keepdims=True))
            alpha[c] = jnp.exp2(ms[c] - m_new)
            p[c] = jnp.exp2(s - m_new).astype(BF16)
            ms[c] = m_new
        for c in chains:
            pv = jnp.dot(vtj, p[c], preferred_element_type=F32)
            acc_ref[:, cols[c]] = alpha[c] * acc_ref[:, cols[c]] + pv
        return tuple(ms)

    def pair(t, ms):
        scores(2 * t + 1, 1, None)
        ms = softmax_pv(2 * t, 0, ms, None)
        scores(2 * t + 2, 0, None)
        return softmax_pv(2 * t + 1, 1, ms, None)

    blocks_per_q = TQ // TK
    first_diag = i * blocks_per_q
    ms = tuple(jnp.full((1, TC), -jnp.inf, F32) for _ in all_chains)
    scores(0, 0, None)
    ms = lax.fori_loop(0, first_diag // 2, pair, ms)
    for r in range(blocks_per_q):
        if r + 1 < blocks_per_q:
            scores(first_diag + r + 1, (r + 1) % 2, r + 1)
        ms = softmax_pv(first_diag + r, r % 2, ms, r)

    prod = lq_ref[...] * lk_ref[...]
    e = jnp.exp(jnp.sum(prod, axis=-1, keepdims=True))
    lam = e[0:1, :] - e[1:2, :] + lam_init
    for r in range(n_half):
        c1, c2 = r, n_half + r
        o1 = acc_ref[:dv, cols[c1]] * (1.0 / acc_ref[dv:dv + 1, cols[c1]])
        o2 = acc_ref[:dv, cols[c2]] * (1.0 / acc_ref[dv:dv + 1, cols[c2]])
        o = (o1 - lam * o2).T
        o = _rms_scale(o) * subg_ref[...]
        o_ref[r * TC:(r + 1) * TC, :] = (o * (1.0 - lam_init)).astype(BF16)


def _diff_attention(qt, k, vt, batch, seq, lam_q, lam_k, sub_g, lam_init):
    assert TC == TK and TQ % (2 * TK) == 0
    nq = seq // TQ
    return pl.pallas_call(
        functools.partial(_attn_kernel, lam_init=lam_init),
        grid=(batch, DIFF_HEADS, nq),
        in_specs=[
            pl.BlockSpec((1, LANES, TQ), lambda b, h, i: (b, h, i)),
            pl.BlockSpec((seq, LANES), lambda b, h, i: (b, h)),
            pl.BlockSpec((1, seq // TK, VT_ROWS, TK), lambda b, h, i: (b, 0, h, 0)),
            _resident((2, DIFF_HEAD_DIM)),
            _resident((2, DIFF_HEAD_DIM)),
            _resident((1, DIFF_V_DIM)),
        ],
        out_specs=pl.BlockSpec((TQ, LANES), lambda b, h, i: (b * nq + i, h)),
        out_shape=jax.ShapeDtypeStruct((batch * seq, D_V), BF16),
        scratch_shapes=[pltpu.VMEM((LANES, 2 * TQ), BF16), pltpu.VMEM((VT_ROWS, 2 * TQ), F32),
                        pltpu.VMEM((2, TK, 2 * TQ), F32)],
        compiler_params=_params(),
        name="diff_attn",
    )(qt, k, vt, lam_q, lam_k, sub_g)


def kernel(x, attn_norm_g, ffn_norm_g, gmlp_w_in, gmlp_ln_g, gmlp_ln_b, gmlp_w_s, gmlp_b_s, gmlp_w_out, kv_norm_g, w_kv, diff_w_q, diff_lambda_q, diff_lambda_k, diff_sub_g, diff_w_o, ffn_w_gu, ffn_w_down, final_norm_g):
    batch, seq, _ = x.shape
    depth = attn_norm_g.shape[0]
    assert depth == 2 and gmlp_w_in.shape[0] == N_A and diff_w_q.shape[0] == depth - N_A
    row = lambda a: a.reshape(1, -1)
    h = x.reshape(batch * seq, D_MODEL)

    h = _gmlp_layer(h, row(attn_norm_g[0]), gmlp_w_in[0].astype(BF16), row(gmlp_ln_g[0]),
                    row(gmlp_ln_b[0]), gmlp_w_s[0], gmlp_b_s[0].T, gmlp_w_out[0].astype(BF16))
    h = _ffn_layer(h, row(ffn_norm_g[0]), ffn_w_gu[0].astype(BF16), ffn_w_down[0].astype(BF16))

    qt, k, vt = _qkv_proj(h, batch, seq, row(attn_norm_g[1]), row(kv_norm_g),
                          diff_w_q[0].astype(BF16), w_kv.astype(BF16))
    lam_init = 0.8 - 0.6 * math.exp(-0.3 * 1)
    attn = _diff_attention(qt, k, vt, batch, seq, diff_lambda_q[0], diff_lambda_k[0],
                           row(diff_sub_g[0]), lam_init)
    h = _ffn_layer(h, row(ffn_norm_g[1]), ffn_w_gu[1].astype(BF16), ffn_w_down[1].astype(BF16),
                   attn=attn, w_o=diff_w_o[0].astype(BF16), final_g=row(final_norm_g))
    return h.reshape(batch, seq, D_MODEL)
```

```python
import functools
import math

import jax
import jax.numpy as jnp
from jax import lax
from jax.experimental import pallas as pl
from jax.experimental.pallas import tpu as pltpu

D_MODEL = 1024
CHUNK = 128
GMLP_WIDTH = 2 * D_MODEL
GMLP_GROUPS = 8
GMLP_GROUP_DIM = GMLP_WIDTH // GMLP_GROUPS
DIFF_HEADS = 8
DIFF_HEAD_DIM = D_MODEL // (2 * DIFF_HEADS)
DIFF_V_DIM = 2 * DIFF_HEAD_DIM
D_K = DIFF_HEADS * 2 * DIFF_HEAD_DIM
D_V = DIFF_HEADS * DIFF_V_DIM
ROT_DIM = DIFF_HEAD_DIM // 4
ROPE_THETA = 500000.0
D_FF = -(-(8 * D_MODEL) // (3 * 256)) * 256
EPS = 1e-5
N_A = 1

LANES = 128
VMEM_LIMIT_BYTES = 56 * 2**20

TM_GMLP = 512
TM_FFN = 256
TM_QKV = 512
TQ = 1024
TK = 256
TC = 256
SCORE_SLOTS = TQ // TK
LOOKAHEAD = 2
BF16_SUBLANES = 16
VT_ROWS = DIFF_V_DIM + BF16_SUBLANES

F32 = jnp.float32
BF16 = jnp.bfloat16


def _rms_scale(x):
    return x * lax.rsqrt(jnp.mean(x * x, axis=-1, keepdims=True) + EPS)


def _gelu(x):
    return 0.5 * x * (1.0 + lax.erf(x * math.sqrt(0.5)))


def _resident(shape):
    return pl.BlockSpec(shape, lambda *_: (0,) * len(shape), pipeline_mode=pl.Buffered(1))


def _params():
    return pltpu.CompilerParams(vmem_limit_bytes=VMEM_LIMIT_BYTES)


def _gmlp_kernel(x_ref, g_ref, win_ref, lng_ref, lnb_ref, ws_ref, bs_ref, wout_ref, o_ref,
                 vn_ref, gated_ref):
    x = x_ref[...]
    xn = (_rms_scale(x) * g_ref[...]).astype(BF16)

    v = jnp.dot(xn, win_ref[:, GMLP_WIDTH:], preferred_element_type=F32)
    v = _gelu(v)
    mu = jnp.mean(v, axis=-1, keepdims=True)
    vc = v - mu
    vn = vc * lax.rsqrt(jnp.mean(vc * vc, axis=-1, keepdims=True) + EPS)
    vn_ref[...] = (vn * lng_ref[...] + lnb_ref[...]).astype(BF16)

    row = lax.broadcasted_iota(jnp.int32, (CHUNK, CHUNK), 0)
    col = lax.broadcasted_iota(jnp.int32, (CHUNK, CHUNK), 1)
    causal = col <= row
    n_chunks = x.shape[0] // CHUNK
    for g in range(GMLP_GROUPS):
        cols = slice(g * GMLP_GROUP_DIM, (g + 1) * GMLP_GROUP_DIM)
        u = jnp.dot(xn, win_ref[:, cols], preferred_element_type=F32)
        u = _gelu(u)
        ws = jnp.where(causal, ws_ref[g], 0.0).astype(BF16)
        bias = bs_ref[:, g:g + 1]
        for c in range(n_chunks):
            rows = slice(c * CHUNK, (c + 1) * CHUNK)
            mixed = jnp.dot(ws, vn_ref[rows, cols], preferred_element_type=F32) + bias
            gated_ref[rows, cols] = (u[rows, :] * mixed).astype(BF16)

    o_ref[...] = x + jnp.dot(gated_ref[...], wout_ref[...], preferred_element_type=F32)


def _gmlp_layer(x2d, norm_g, w_in, ln_g, ln_b, w_s, b_s_t, w_out):
    t = x2d.shape[0]
    tm = TM_GMLP
    row = lambda i: (i, 0)
    return pl.pallas_call(
        _gmlp_kernel,
        grid=(t // tm,),
        in_specs=[
            pl.BlockSpec((tm, D_MODEL), row),
            _resident((1, D_MODEL)),
            _resident((D_MODEL, 2 * GMLP_WIDTH)),
            _resident((1, GMLP_WIDTH)),
            _resident((1, GMLP_WIDTH)),
            _resident((GMLP_GROUPS, CHUNK, CHUNK)),
            _resident((CHUNK, GMLP_GROUPS)),
            _resident((GMLP_WIDTH, D_MODEL)),
        ],
        out_specs=pl.BlockSpec((tm, D_MODEL), row),
        out_shape=jax.ShapeDtypeStruct((t, D_MODEL), F32),
        scratch_shapes=[pltpu.VMEM((tm, GMLP_WIDTH), BF16), pltpu.VMEM((tm, GMLP_WIDTH), BF16)],
        compiler_params=_params(),
        name="gmlp",
    )(x2d, norm_g, w_in, ln_g, ln_b, w_s, b_s_t, w_out)


def _ffn_kernel(*refs, has_oproj, has_final_norm):
    refs = list(refs)
    h_ref = refs.pop(0)
    if has_oproj:
        a_ref = refs.pop(0)
        wo_ref = refs.pop(0)
    g_ref, wgu_ref, wd_ref = refs.pop(0), refs.pop(0), refs.pop(0)
    if has_final_norm:
        fg_ref = refs.pop(0)
    (o_ref,) = refs

    h = h_ref[...]
    if has_oproj:
        h = h + jnp.dot(a_ref[...], wo_ref[...], preferred_element_type=F32)
    xn = (_rms_scale(h) * g_ref[...]).astype(BF16)
    gate = jnp.dot(xn, wgu_ref[:, :D_FF], preferred_element_type=F32)
    up = jnp.dot(xn, wgu_ref[:, D_FF:], preferred_element_type=F32)
    act = (jax.nn.silu(gate) * up).astype(BF16)
    h = h + jnp.dot(act, wd_ref[...], preferred_element_type=F32)
    if has_final_norm:
        h = _rms_scale(h) * fg_ref[...]
    o_ref[...] = h


def _ffn_layer(h2d, norm_g, w_gu, w_down, attn=None, w_o=None, final_g=None):
    t = h2d.shape[0]
    tm = TM_FFN
    row = lambda i: (i, 0)
    has_oproj = attn is not None
    has_final_norm = final_g is not None
    args = [h2d]
    specs = [pl.BlockSpec((tm, D_MODEL), row)]
    if has_oproj:
        args += [attn, w_o]
        specs += [pl.BlockSpec((tm, D_V), row), _resident((D_V, D_MODEL))]
    args += [norm_g, w_gu, w_down]
    specs += [_resident((1, D_MODEL)), _resident((D_MODEL, 2 * D_FF)), _resident((D_FF, D_MODEL))]
    if has_final_norm:
        args.append(final_g)
        specs.append(_resident((1, D_MODEL)))
    return pl.pallas_call(
        functools.partial(_ffn_kernel, has_oproj=has_oproj, has_final_norm=has_final_norm),
        grid=(t // tm,),
        in_specs=specs,
        out_specs=pl.BlockSpec((tm, D_MODEL), row),
        out_shape=jax.ShapeDtypeStruct((t, D_MODEL), F32),
        compiler_params=_params(),
        name="ffn_final" if has_final_norm else "ffn",
    )(*args)


def _rope(x, ra, rb, rc):
    half = ROT_DIM // 2
    outs = []
    for j in range(x.shape[1] // LANES):
        xs = x[:, j * LANES:(j + 1) * LANES]
        fwd = pltpu.roll(xs, LANES - half, axis=1)
        bwd = pltpu.roll(xs, half, axis=1)
        outs.append(xs * ra + fwd * rb + bwd * rc)
    return outs


def _qkv_kernel(h_ref, gq_ref, gkv_ref, wq_ref, wkv_ref, ra_ref, rb_ref, rc_ref,
                qt_ref, k_ref, vt_ref):
    y = _rms_scale(h_ref[...])
    xq = (y * gq_ref[...]).astype(BF16)
    xkv = (y * gkv_ref[...]).astype(BF16)
    ra, rb, rc = ra_ref[...], rb_ref[...], rc_ref[...]
    q = jnp.dot(xq, wq_ref[...], preferred_element_type=F32)
    scale = DIFF_HEAD_DIM ** -0.5 * math.log2(math.e)
    for j, qs in enumerate(_rope(q, ra, rb, rc)):
        qt_ref[0, j * LANES:(j + 1) * LANES, :] = (qs * scale).T.astype(BF16)
    k = jnp.dot(xkv, wkv_ref[:, :D_K], preferred_element_type=F32)
    for j, ks in enumerate(_rope(k, ra, rb, rc)):
        k_ref[:, j * LANES:(j + 1) * LANES] = ks.astype(BF16)
    v = jnp.dot(xkv, wkv_ref[:, D_K:], preferred_element_type=F32)
    ones = jnp.ones((VT_ROWS - DIFF_V_DIM, TK), BF16)
    for t in range(v.shape[0] // TK):
        for j in range(DIFF_HEADS):
            vt_ref[0, t, j * VT_ROWS:j * VT_ROWS + DIFF_V_DIM, :] = (
                v[t * TK:(t + 1) * TK, j * DIFF_V_DIM:(j + 1) * DIFF_V_DIM].T.astype(BF16))
            vt_ref[0, t, j * VT_ROWS + DIFF_V_DIM:(j + 1) * VT_ROWS, :] = ones


def _rope_tables(seq):
    half = ROT_DIM // 2
    inv_freq = jnp.power(ROPE_THETA, -jnp.arange(half, dtype=F32) * 2.0 / ROT_DIM)
    ang = jnp.arange(seq, dtype=jnp.int32).astype(F32)[:, None] * inv_freq[None, :]
    cos, sin = jnp.cos(ang), jnp.sin(ang)
    ones = jnp.ones((seq, DIFF_HEAD_DIM - ROT_DIM), F32)
    zeros_h = jnp.zeros((seq, half), F32)
    zeros_r = jnp.zeros((seq, DIFF_HEAD_DIM - ROT_DIM), F32)
    ra = jnp.concatenate([cos, cos, ones], axis=1)
    rb = jnp.concatenate([-sin, zeros_h, zeros_r], axis=1)
    rc = jnp.concatenate([zeros_h, sin, zeros_r], axis=1)
    rep = LANES // DIFF_HEAD_DIM
    return tuple(jnp.tile(a, (1, rep)) for a in (ra, rb, rc))


def _qkv_proj(h2d, batch, seq, gq, gkv, w_q, w_kv):
    tm = TM_QKV
    assert tm % TK == 0
    nb = seq // tm
    row = lambda b, i: (b * nb + i, 0)
    pos = lambda b, i: (i, 0)
    col = lambda b, i: (b, 0, i)
    ra, rb, rc = _rope_tables(seq)
    return pl.pallas_call(
        _qkv_kernel,
        grid=(batch, nb),
        in_specs=[
            pl.BlockSpec((tm, D_MODEL), row),
            _resident((1, D_MODEL)),
            _resident((1, D_MODEL)),
            _resident((D_MODEL, D_K)),
            _resident((D_MODEL, D_K + D_V)),
            pl.BlockSpec((tm, LANES), pos),
            pl.BlockSpec((tm, LANES), pos),
            pl.BlockSpec((tm, LANES), pos),
        ],
        out_specs=[pl.BlockSpec((1, D_K, tm), col), pl.BlockSpec((tm, D_K), row),
                   pl.BlockSpec((1, tm // TK, DIFF_HEADS * VT_ROWS, TK), lambda b, i: (b, i, 0, 0))],
        out_shape=[jax.ShapeDtypeStruct((batch, D_K, seq), BF16),
                   jax.ShapeDtypeStruct((batch * seq, D_K), BF16),
                   jax.ShapeDtypeStruct((batch, seq // TK, DIFF_HEADS * VT_ROWS, TK), BF16)],
        compiler_params=_params(),
        name="qkv",
    )(h2d, gq, gkv, w_q, w_kv, ra, rb, rc)


def _attn_kernel(qt_ref, k_ref, vt_ref, lq_ref, lk_ref, subg_ref, o_ref, qq_ref, acc_ref, s_ref, *,
                 lam_init):
    i = pl.program_id(2)
    n_half = TQ // TC
    d = DIFF_HEAD_DIM
    dv = DIFF_V_DIM

    qt = qt_ref[0]
    zero = jnp.zeros((d, TQ), BF16)
    qq_ref[:d, :TQ] = qt[:d, :]
    qq_ref[d:, :TQ] = zero
    qq_ref[:d, TQ:] = zero
    qq_ref[d:, TQ:] = qt[d:, :]
    acc_ref[...] = jnp.zeros(acc_ref.shape, F32)

    key_idx = lax.broadcasted_iota(jnp.int32, (TK, TC), 0)
    qry_idx = lax.broadcasted_iota(jnp.int32, (TK, TC), 1)
    causal = key_idx <= qry_idx

    all_chains = list(range(2 * n_half))
    cols = {c: slice(c * TC, (c + 1) * TC) for c in all_chains}

    def chains_from(first_chain_row):
        return [c for c in all_chains if first_chain_row is None or c % n_half >= first_chain_row]

    def scores(j, slot, first_chain_row):
        kj = k_ref[pl.ds(pl.multiple_of(j * TK, TK), TK), :]
        col_max = [None] * len(all_chains)
        for c in chains_from(first_chain_row):
            s = jnp.dot(kj, qq_ref[:, cols[c]], preferred_element_type=F32)
            s_ref[slot, :, cols[c]] = s
            col_max[c] = jnp.max(s, axis=0, keepdims=True)
        return tuple(col_max)

    def softmax_pv(j, slot, ms, col_max, first_chain_row):
        vtj = vt_ref[0, j]
        ms = list(ms)
        chains = chains_from(first_chain_row)
        p, alpha = {}, {}
        for c in chains:
            s = s_ref[slot, :, cols[c]]
            mx = col_max[c]
            if first_chain_row is not None and c % n_half == first_chain_row:
                s = jnp.where(causal, s, -jnp.inf)
                mx = jnp.max(s, axis=0, keepdims=True)
            m_new = jnp.maximum(ms[c], mx)
            alpha[c] = jnp.exp2(ms[c] - m_new)
            p[c] = jnp.exp2(s - m_new).astype(BF16)
            ms[c] = m_new
        for c in chains:
            pv = jnp.dot(vtj, p[c], preferred_element_type=F32)
            acc_ref[:, cols[c]] = alpha[c] * acc_ref[:, cols[c]] + pv
        return tuple(ms)

    def earlier_query_block(qb, carry):
        ms, ahead = carry
        ahead = list(ahead)
        b0 = qb * SCORE_SLOTS
        for u in range(SCORE_SLOTS):
            ahead.append(scores(b0 + u + LOOKAHEAD, (u + LOOKAHEAD) % SCORE_SLOTS, None))
            ms = softmax_pv(b0 + u, u, ms, ahead.pop(0), None)
        return ms, tuple(ahead)

    first_diag = i * SCORE_SLOTS
    ms = tuple(jnp.full((1, TC), -jnp.inf, F32) for _ in all_chains)
    ahead = tuple(scores(u, u, None) for u in range(LOOKAHEAD))
    ms, ahead = lax.fori_loop(0, i, earlier_query_block, (ms, ahead))
    ahead = list(ahead)
    for r in range(SCORE_SLOTS):
        if r + LOOKAHEAD < SCORE_SLOTS:
            ahead.append(scores(first_diag + r + LOOKAHEAD, r + LOOKAHEAD, r + LOOKAHEAD))
        ms = softmax_pv(first_diag + r, r, ms, ahead.pop(0), r)

    prod = lq_ref[...] * lk_ref[...]
    e = jnp.exp(jnp.sum(prod, axis=-1, keepdims=True))
    lam = e[0:1, :] - e[1:2, :] + lam_init
    for r in range(n_half):
        c1, c2 = r, n_half + r
        o1 = acc_ref[:dv, cols[c1]] * (1.0 / acc_ref[dv:dv + 1, cols[c1]])
        o2 = acc_ref[:dv, cols[c2]] * (1.0 / acc_ref[dv:dv + 1, cols[c2]])
        o = (o1 - lam * o2).T
        o = _rms_scale(o) * subg_ref[...]
        o_ref[r * TC:(r + 1) * TC, :] = (o * (1.0 - lam_init)).astype(BF16)


def _diff_attention(qt, k, vt, batch, seq, lam_q, lam_k, sub_g, lam_init):
    assert TC == TK and TQ == SCORE_SLOTS * TK and 0 < LOOKAHEAD < SCORE_SLOTS
    nq = seq // TQ
    return pl.pallas_call(
        functools.partial(_attn_kernel, lam_init=lam_init),
        grid=(batch, DIFF_HEADS, nq),
        in_specs=[
            pl.BlockSpec((1, LANES, TQ), lambda b, h, i: (b, h, i)),
            pl.BlockSpec((seq, LANES), lambda b, h, i: (b, h)),
            pl.BlockSpec((1, seq // TK, VT_ROWS, TK), lambda b, h, i: (b, 0, h, 0)),
            _resident((2, DIFF_HEAD_DIM)),
            _resident((2, DIFF_HEAD_DIM)),
            _resident((1, DIFF_V_DIM)),
        ],
        out_specs=pl.BlockSpec((TQ, LANES), lambda b, h, i: (b * nq + i, h)),
        out_shape=jax.ShapeDtypeStruct((batch * seq, D_V), BF16),
        scratch_shapes=[pltpu.VMEM((LANES, 2 * TQ), BF16), pltpu.VMEM((VT_ROWS, 2 * TQ), F32),
                        pltpu.VMEM((SCORE_SLOTS, TK, 2 * TQ), F32)],
        compiler_params=_params(),
        name="diff_attn",
    )(qt, k, vt, lam_q, lam_k, sub_g)


def kernel(x, attn_norm_g, ffn_norm_g, gmlp_w_in, gmlp_ln_g, gmlp_ln_b, gmlp_w_s, gmlp_b_s, gmlp_w_out, kv_norm_g, w_kv, diff_w_q, diff_lambda_q, diff_lambda_k, diff_sub_g, diff_w_o, ffn_w_gu, ffn_w_down, final_norm_g):
    batch, seq, _ = x.shape
    depth = attn_norm_g.shape[0]
    assert depth == 2 and gmlp_w_in.shape[0] == N_A and diff_w_q.shape[0] == depth - N_A
    row = lambda a: a.reshape(1, -1)
    h = x.reshape(batch * seq, D_MODEL)

    h = _gmlp_layer(h, row(attn_norm_g[0]), gmlp_w_in[0].astype(BF16), row(gmlp_ln_g[0]),
                    row(gmlp_ln_b[0]), gmlp_w_s[0], gmlp_b_s[0].T, gmlp_w_out[0].astype(BF16))
    h = _ffn_layer(h, row(ffn_norm_g[0]), ffn_w_gu[0].astype(BF16), ffn_w_down[0].astype(BF16))

    qt, k, vt = _qkv_proj(h, batch, seq, row(attn_norm_g[1]), row(kv_norm_g),
                          diff_w_q[0].astype(BF16), w_kv.astype(BF16))
    lam_init = 0.8 - 0.6 * math.exp(-0.3 * 1)
    attn = _diff_attention(qt, k, vt, batch, seq, diff_lambda_q[0], diff_lambda_k[0],
                           row(diff_sub_g[0]), lam_init)
    h = _ffn_layer(h, row(ffn_norm_g[1]), ffn_w_gu[1].astype(BF16), ffn_w_down[1].astype(BF16),
                   attn=attn, w_o=diff_w_o[0].astype(BF16), final_g=row(final_norm_g))
    return h.reshape(batch, seq, D_MODEL)
```

```python
import functools
import math

import jax
import jax.numpy as jnp
from jax import lax
from jax.experimental import pallas as pl
from jax.experimental.pallas import tpu as pltpu

D_MODEL = 1024
CHUNK = 128
GMLP_WIDTH = 2 * D_MODEL
GMLP_GROUPS = 8
GMLP_GROUP_DIM = GMLP_WIDTH // GMLP_GROUPS
DIFF_HEADS = 8
DIFF_HEAD_DIM = D_MODEL // (2 * DIFF_HEADS)
DIFF_V_DIM = 2 * DIFF_HEAD_DIM
D_K = DIFF_HEADS * 2 * DIFF_HEAD_DIM
D_V = DIFF_HEADS * DIFF_V_DIM
ROT_DIM = DIFF_HEAD_DIM // 4
ROPE_THETA = 500000.0
D_FF = -(-(8 * D_MODEL) // (3 * 256)) * 256
EPS = 1e-5
N_A = 1

LANES = 128
VMEM_LIMIT_BYTES = 56 * 2**20

TM_GMLP = 512
TM_FFN = 512
TM_QKV = 1024
TQ = 1024
TK = 256
TC = 256
SCORE_SLOTS = TQ // TK
LOOKAHEAD = 2
BF16_SUBLANES = 16
VT_ROWS = DIFF_V_DIM + BF16_SUBLANES

F32 = jnp.float32
BF16 = jnp.bfloat16


def _rms_scale(x):
    return x * lax.rsqrt(jnp.mean(x * x, axis=-1, keepdims=True) + EPS)


def _gelu(x):
    return 0.5 * x * (1.0 + lax.erf(x * math.sqrt(0.5)))


def _resident(shape):
    return pl.BlockSpec(shape, lambda *_: (0,) * len(shape), pipeline_mode=pl.Buffered(1))


def _params():
    return pltpu.CompilerParams(vmem_limit_bytes=VMEM_LIMIT_BYTES)


def _gmlp_kernel(x_ref, g_ref, win_ref, lng_ref, lnb_ref, ws_ref, bs_ref, wout_ref, o_ref,
                 vn_ref, gated_ref):
    x = x_ref[...]
    xn = (_rms_scale(x) * g_ref[...]).astype(BF16)

    v = jnp.dot(xn, win_ref[:, GMLP_WIDTH:], preferred_element_type=F32)
    v = _gelu(v)
    mu = jnp.mean(v, axis=-1, keepdims=True)
    vc = v - mu
    vn = vc * lax.rsqrt(jnp.mean(vc * vc, axis=-1, keepdims=True) + EPS)
    vn_ref[...] = (vn * lng_ref[...] + lnb_ref[...]).astype(BF16)

    row = lax.broadcasted_iota(jnp.int32, (CHUNK, CHUNK), 0)
    col = lax.broadcasted_iota(jnp.int32, (CHUNK, CHUNK), 1)
    causal = col <= row
    n_chunks = x.shape[0] // CHUNK
    for g in range(GMLP_GROUPS):
        cols = slice(g * GMLP_GROUP_DIM, (g + 1) * GMLP_GROUP_DIM)
        u = jnp.dot(xn, win_ref[:, cols], preferred_element_type=F32)
        u = _gelu(u)
        ws = jnp.where(causal, ws_ref[g], 0.0).astype(BF16)
        bias = bs_ref[:, g:g + 1]
        for c in range(n_chunks):
            rows = slice(c * CHUNK, (c + 1) * CHUNK)
            mixed = jnp.dot(ws, vn_ref[rows, cols], preferred_element_type=F32) + bias
            gated_ref[rows, cols] = (u[rows, :] * mixed).astype(BF16)

    o_ref[...] = x + jnp.dot(gated_ref[...], wout_ref[...], preferred_element_type=F32)


def _gmlp_layer(x2d, norm_g, w_in, ln_g, ln_b, w_s, b_s_t, w_out):
    t = x2d.shape[0]
    tm = TM_GMLP
    row = lambda i: (i, 0)
    return pl.pallas_call(
        _gmlp_kernel,
        grid=(t // tm,),
        in_specs=[
            pl.BlockSpec((tm, D_MODEL), row),
            _resident((1, D_MODEL)),
            _resident((D_MODEL, 2 * GMLP_WIDTH)),
            _resident((1, GMLP_WIDTH)),
            _resident((1, GMLP_WIDTH)),
            _resident((GMLP_GROUPS, CHUNK, CHUNK)),
            _resident((CHUNK, GMLP_GROUPS)),
            _resident((GMLP_WIDTH, D_MODEL)),
        ],
        out_specs=pl.BlockSpec((tm, D_MODEL), row),
        out_shape=jax.ShapeDtypeStruct((t, D_MODEL), F32),
        scratch_shapes=[pltpu.VMEM((tm, GMLP_WIDTH), BF16), pltpu.VMEM((tm, GMLP_WIDTH), BF16)],
        compiler_params=_params(),
        name="gmlp",
    )(x2d, norm_g, w_in, ln_g, ln_b, w_s, b_s_t, w_out)


def _ffn_kernel(*refs, has_oproj, has_final_norm):
    refs = list(refs)
    h_ref = refs.pop(0)
    if has_oproj:
        a_ref = refs.pop(0)
        wo_ref = refs.pop(0)
    g_ref, wgu_ref, wd_ref = refs.pop(0), refs.pop(0), refs.pop(0)
    if has_final_norm:
        fg_ref = refs.pop(0)
    (o_ref,) = refs

    h = h_ref[...]
    if has_oproj:
        h = h + jnp.dot(a_ref[...], wo_ref[...], preferred_element_type=F32)
    xn = (_rms_scale(h) * g_ref[...]).astype(BF16)
    gate = jnp.dot(xn, wgu_ref[:, :D_FF], preferred_element_type=F32)
    up = jnp.dot(xn, wgu_ref[:, D_FF:], preferred_element_type=F32)
    act = (jax.nn.silu(gate) * up).astype(BF16)
    h = h + jnp.dot(act, wd_ref[...], preferred_element_type=F32)
    if has_final_norm:
        h = _rms_scale(h) * fg_ref[...]
    o_ref[...] = h


def _ffn_layer(h2d, norm_g, w_gu, w_down, attn=None, w_o=None, final_g=None):
    t = h2d.shape[0]
    tm = TM_FFN
    row = lambda i: (i, 0)
    has_oproj = attn is not None
    has_final_norm = final_g is not None
    args = [h2d]
    specs = [pl.BlockSpec((tm, D_MODEL), row)]
    if has_oproj:
        args += [attn, w_o]
        specs += [pl.BlockSpec((tm, D_V), row), _resident((D_V, D_MODEL))]
    args += [norm_g, w_gu, w_down]
    specs += [_resident((1, D_MODEL)), _resident((D_MODEL, 2 * D_FF)), _resident((D_FF, D_MODEL))]
    if has_final_norm:
        args.append(final_g)
        specs.append(_resident((1, D_MODEL)))
    return pl.pallas_call(
        functools.partial(_ffn_kernel, has_oproj=has_oproj, has_final_norm=has_final_norm),
        grid=(t // tm,),
        in_specs=specs,
        out_specs=pl.BlockSpec((tm, D_MODEL), row),
        out_shape=jax.ShapeDtypeStruct((t, D_MODEL), F32),
        compiler_params=_params(),
        name="ffn_final" if has_final_norm else "ffn",
    )(*args)


def _rope(x, ra, rb, rc):
    half = ROT_DIM // 2
    outs = []
    for j in range(x.shape[1] // LANES):
        xs = x[:, j * LANES:(j + 1) * LANES]
        fwd = pltpu.roll(xs, LANES - half, axis=1)
        bwd = pltpu.roll(xs, half, axis=1)
        outs.append(xs * ra + fwd * rb + bwd * rc)
    return outs


def _qkv_kernel(h_ref, gq_ref, gkv_ref, wq_ref, wkv_ref, ra_ref, rb_ref, rc_ref,
                qt_ref, k_ref, vt_ref):
    y = _rms_scale(h_ref[...])
    xq = (y * gq_ref[...]).astype(BF16)
    xkv = (y * gkv_ref[...]).astype(BF16)
    ra, rb, rc = ra_ref[...], rb_ref[...], rc_ref[...]
    q = jnp.dot(xq, wq_ref[...], preferred_element_type=F32)
    scale = DIFF_HEAD_DIM ** -0.5 * math.log2(math.e)
    for j, qs in enumerate(_rope(q, ra, rb, rc)):
        qt_ref[0, j * LANES:(j + 1) * LANES, :] = (qs * scale).T.astype(BF16)
    k = jnp.dot(xkv, wkv_ref[:, :D_K], preferred_element_type=F32)
    for j, ks in enumerate(_rope(k, ra, rb, rc)):
        k_ref[:, j * LANES:(j + 1) * LANES] = ks.astype(BF16)
    v = jnp.dot(xkv, wkv_ref[:, D_K:], preferred_element_type=F32)
    ones = jnp.ones((VT_ROWS - DIFF_V_DIM, TK), BF16)
    for t in range(v.shape[0] // TK):
        for j in range(DIFF_HEADS):
            vt_ref[0, t, j * VT_ROWS:j * VT_ROWS + DIFF_V_DIM, :] = (
                v[t * TK:(t + 1) * TK, j * DIFF_V_DIM:(j + 1) * DIFF_V_DIM].T.astype(BF16))
            vt_ref[0, t, j * VT_ROWS + DIFF_V_DIM:(j + 1) * VT_ROWS, :] = ones


def _rope_tables(seq):
    half = ROT_DIM // 2
    inv_freq = jnp.power(ROPE_THETA, -jnp.arange(half, dtype=F32) * 2.0 / ROT_DIM)
    ang = jnp.arange(seq, dtype=jnp.int32).astype(F32)[:, None] * inv_freq[None, :]
    cos, sin = jnp.cos(ang), jnp.sin(ang)
    ones = jnp.ones((seq, DIFF_HEAD_DIM - ROT_DIM), F32)
    zeros_h = jnp.zeros((seq, half), F32)
    zeros_r = jnp.zeros((seq, DIFF_HEAD_DIM - ROT_DIM), F32)
    ra = jnp.concatenate([cos, cos, ones], axis=1)
    rb = jnp.concatenate([-sin, zeros_h, zeros_r], axis=1)
    rc = jnp.concatenate([zeros_h, sin, zeros_r], axis=1)
    rep = LANES // DIFF_HEAD_DIM
    return tuple(jnp.tile(a, (1, rep)) for a in (ra, rb, rc))


def _qkv_proj(h2d, batch, seq, gq, gkv, w_q, w_kv):
    tm = TM_QKV
    assert tm % TK == 0
    nb = seq // tm
    row = lambda b, i: (b * nb + i, 0)
    pos = lambda b, i: (i, 0)
    col = lambda b, i: (b, 0, i)
    ra, rb, rc = _rope_tables(seq)
    return pl.pallas_call(
        _qkv_kernel,
        grid=(batch, nb),
        in_specs=[
            pl.BlockSpec((tm, D_MODEL), row),
            _resident((1, D_MODEL)),
            _resident((1, D_MODEL)),
            _resident((D_MODEL, D_K)),
            _resident((D_MODEL, D_K + D_V)),
            pl.BlockSpec((tm, LANES), pos),
            pl.BlockSpec((tm, LANES), pos),
            pl.BlockSpec((tm, LANES), pos),
        ],
        out_specs=[pl.BlockSpec((1, D_K, tm), col), pl.BlockSpec((tm, D_K), row),
                   pl.BlockSpec((1, tm // TK, DIFF_HEADS * VT_ROWS, TK), lambda b, i: (b, i, 0, 0))],
        out_shape=[jax.ShapeDtypeStruct((batch, D_K, seq), BF16),
                   jax.ShapeDtypeStruct((batch * seq, D_K), BF16),
                   jax.ShapeDtypeStruct((batch, seq // TK, DIFF_HEADS * VT_ROWS, TK), BF16)],
        compiler_params=_params(),
        name="qkv",
    )(h2d, gq, gkv, w_q, w_kv, ra, rb, rc)


def _attn_kernel(qt_ref, k_ref, vt_ref, lq_ref, lk_ref, subg_ref, o_ref, qq_ref, acc_ref, s_ref, *,
                 lam_init):
    i = pl.program_id(2)
    n_half = TQ // TC
    d = DIFF_HEAD_DIM
    dv = DIFF_V_DIM

    qt = qt_ref[0]
    zero = jnp.zeros((d, TQ), BF16)
    qq_ref[:d, :TQ] = qt[:d, :]
    qq_ref[d:, :TQ] = zero
    qq_ref[:d, TQ:] = zero
    qq_ref[d:, TQ:] = qt[d:, :]
    acc_ref[...] = jnp.zeros(acc_ref.shape, F32)

    key_idx = lax.broadcasted_iota(jnp.int32, (TK, TC), 0)
    qry_idx = lax.broadcasted_iota(jnp.int32, (TK, TC), 1)
    causal = key_idx <= qry_idx

    all_chains = list(range(2 * n_half))
    cols = {c: slice(c * TC, (c + 1) * TC) for c in all_chains}

    def chains_from(first_chain_row):
        return [c for c in all_chains if first_chain_row is None or c % n_half >= first_chain_row]

    def scores(j, slot, first_chain_row):
        kj = k_ref[pl.ds(pl.multiple_of(j * TK, TK), TK), :]
        col_max = [None] * len(all_chains)
        for c in chains_from(first_chain_row):
            s = jnp.dot(kj, qq_ref[:, cols[c]], preferred_element_type=F32)
            s_ref[slot, :, cols[c]] = s
            col_max[c] = jnp.max(s, axis=0, keepdims=True)
        return tuple(col_max)

    def softmax_pv(j, slot, ms, col_max, first_chain_row):
        vtj = vt_ref[0, j]
        ms = list(ms)
        for c in chains_from(first_chain_row):
            s = s_ref[slot, :, cols[c]]
            mx = col_max[c]
            if first_chain_row is not None and c % n_half == first_chain_row:
                s = jnp.where(causal, s, -jnp.inf)
                mx = jnp.max(s, axis=0, keepdims=True)
            m_new = jnp.maximum(ms[c], mx)
            alpha = jnp.exp2(ms[c] - m_new)
            p = jnp.exp2(s - m_new).astype(BF16)
            ms[c] = m_new
            pv = jnp.dot(vtj, p, preferred_element_type=F32)
            acc_ref[:, cols[c]] = alpha * acc_ref[:, cols[c]] + pv
        return tuple(ms)

    def earlier_query_block(qb, carry):
        ms, ahead = carry
        ahead = list(ahead)
        b0 = qb * SCORE_SLOTS
        for u in range(SCORE_SLOTS):
            ahead.append(scores(b0 + u + LOOKAHEAD, (u + LOOKAHEAD) % SCORE_SLOTS, None))
            ms = softmax_pv(b0 + u, u, ms, ahead.pop(0), None)
        return ms, tuple(ahead)

    first_diag = i * SCORE_SLOTS
    ms = tuple(jnp.full((1, TC), -jnp.inf, F32) for _ in all_chains)
    ahead = tuple(scores(u, u, None) for u in range(LOOKAHEAD))
    ms, ahead = lax.fori_loop(0, i, earlier_query_block, (ms, ahead))
    ahead = list(ahead)
    for r in range(SCORE_SLOTS):
        if r + LOOKAHEAD < SCORE_SLOTS:
            ahead.append(scores(first_diag + r + LOOKAHEAD, r + LOOKAHEAD, r + LOOKAHEAD))
        ms = softmax_pv(first_diag + r, r, ms, ahead.pop(0), r)

    prod = lq_ref[...] * lk_ref[...]
    e = jnp.exp(jnp.sum(prod, axis=-1, keepdims=True))
    lam = e[0:1, :] - e[1:2, :] + lam_init
    for r in range(n_half):
        c1, c2 = r, n_half + r
        o1 = acc_ref[:dv, cols[c1]] * (1.0 / acc_ref[dv:dv + 1, cols[c1]])
        o2 = acc_ref[:dv, cols[c2]] * (1.0 / acc_ref[dv:dv + 1, cols[c2]])
        o = (o1 - lam * o2).T
        o = _rms_scale(o) * subg_ref[...]
        o_ref[r * TC:(r + 1) * TC, :] = (o * (1.0 - lam_init)).astype(BF16)


def _diff_attention(qt, k, vt, batch, seq, lam_q, lam_k, sub_g, lam_init):
    assert TC == TK and TQ == SCORE_SLOTS * TK and 0 < LOOKAHEAD < SCORE_SLOTS
    nq = seq // TQ
    return pl.pallas_call(
        functools.partial(_attn_kernel, lam_init=lam_init),
        grid=(batch, DIFF_HEADS, nq),
        in_specs=[
            pl.BlockSpec((1, LANES, TQ), lambda b, h, i: (b, h, i)),
            pl.BlockSpec((seq, LANES), lambda b, h, i: (b, h)),
            pl.BlockSpec((1, seq // TK, VT_ROWS, TK), lambda b, h, i: (b, 0, h, 0)),
            _resident((2, DIFF_HEAD_DIM)),
            _resident((2, DIFF_HEAD_DIM)),
            _resident((1, DIFF_V_DIM)),
        ],
        out_specs=pl.BlockSpec((TQ, LANES), lambda b, h, i: (b * nq + i, h)),
        out_shape=jax.ShapeDtypeStruct((batch * seq, D_V), BF16),
        scratch_shapes=[pltpu.VMEM((LANES, 2 * TQ), BF16), pltpu.VMEM((VT_ROWS, 2 * TQ), F32),
                        pltpu.VMEM((SCORE_SLOTS, TK, 2 * TQ), F32)],
        compiler_params=_params(),
        name="diff_attn",
    )(qt, k, vt, lam_q, lam_k, sub_g)


def kernel(x, attn_norm_g, ffn_norm_g, gmlp_w_in, gmlp_ln_g, gmlp_ln_b, gmlp_w_s, gmlp_b_s, gmlp_w_out, kv_norm_g, w_kv, diff_w_q, diff_lambda_q, diff_lambda_k, diff_sub_g, diff_w_o, ffn_w_gu, ffn_w_down, final_norm_g):
    batch, seq, _ = x.shape
    depth = attn_norm_g.shape[0]
    assert depth == 2 and gmlp_w_in.shape[0] == N_A and diff_w_q.shape[0] == depth - N_A
    row = lambda a: a.reshape(1, -1)
    h = x.reshape(batch * seq, D_MODEL)

    h = _gmlp_layer(h, row(attn_norm_g[0]), gmlp_w_in[0].astype(BF16), row(gmlp_ln_g[0]),
                    row(gmlp_ln_b[0]), gmlp_w_s[0], gmlp_b_s[0].T, gmlp_w_out[0].astype(BF16))
    h = _ffn_layer(h, row(ffn_norm_g[0]), ffn_w_gu[0].astype(BF16), ffn_w_down[0].astype(BF16))

    qt, k, vt = _qkv_proj(h, batch, seq, row(attn_norm_g[1]), row(kv_norm_g),
                          diff_w_q[0].astype(BF16), w_kv.astype(BF16))
    lam_init = 0.8 - 0.6 * math.exp(-0.3 * 1)
    attn = _diff_attention(qt, k, vt, batch, seq, diff_lambda_q[0], diff_lambda_k[0],
                           row(diff_sub_g[0]), lam_init)
    h = _ffn_layer(h, row(ffn_norm_g[1]), ffn_w_gu[1].astype(BF16), ffn_w_down[1].astype(BF16),
                   attn=attn, w_o=diff_w_o[0].astype(BF16), final_g=row(final_norm_g))
    return h.reshape(batch, seq, D_MODEL)
```

```python
import functools
import math

import jax
import jax.numpy as jnp
from jax import lax
from jax.experimental import pallas as pl
from jax.experimental.pallas import tpu as pltpu

D_MODEL = 1024
CHUNK = 128
GMLP_WIDTH = 2 * D_MODEL
GMLP_GROUPS = 8
GMLP_GROUP_DIM = GMLP_WIDTH // GMLP_GROUPS
DIFF_HEADS = 8
DIFF_HEAD_DIM = D_MODEL // (2 * DIFF_HEADS)
DIFF_V_DIM = 2 * DIFF_HEAD_DIM
D_K = DIFF_HEADS * 2 * DIFF_HEAD_DIM
D_V = DIFF_HEADS * DIFF_V_DIM
ROT_DIM = DIFF_HEAD_DIM // 4
ROPE_THETA = 500000.0
D_FF = -(-(8 * D_MODEL) // (3 * 256)) * 256
EPS = 1e-5
N_A = 1

LANES = 128
VMEM_LIMIT_BYTES = 56 * 2**20

TM_GMLP = 512
TM_FFN = 512
TM_QKV = 1024
TQ = 1024
TK = 256
TC = 256
SCORE_SLOTS = TQ // TK
LOOKAHEAD = 2
BF16_SUBLANES = 16
VT_ROWS = DIFF_V_DIM + BF16_SUBLANES

F32 = jnp.float32
BF16 = jnp.bfloat16


def _rms_scale(x):
    return x * lax.rsqrt(jnp.mean(x * x, axis=-1, keepdims=True) + EPS)


def _gelu(x):
    return 0.5 * x * (1.0 + lax.erf(x * math.sqrt(0.5)))


def _resident(shape):
    return pl.BlockSpec(shape, lambda *_: (0,) * len(shape), pipeline_mode=pl.Buffered(1))


def _params():
    return pltpu.CompilerParams(vmem_limit_bytes=VMEM_LIMIT_BYTES)


def _gmlp_kernel(x_ref, g_ref, win_ref, lng_ref, lnb_ref, ws_ref, bs_ref, wout_ref, o_ref,
                 vn_ref, gated_ref):
    x = x_ref[...]
    xn = (_rms_scale(x) * g_ref[...]).astype(BF16)

    v = jnp.dot(xn, win_ref[:, GMLP_WIDTH:], preferred_element_type=F32)
    v = _gelu(v)
    mu = jnp.mean(v, axis=-1, keepdims=True)
    vc = v - mu
    vn = vc * lax.rsqrt(jnp.mean(vc * vc, axis=-1, keepdims=True) + EPS)
    vn_ref[...] = (vn * lng_ref[...] + lnb_ref[...]).astype(BF16)

    row = lax.broadcasted_iota(jnp.int32, (CHUNK, CHUNK), 0)
    col = lax.broadcasted_iota(jnp.int32, (CHUNK, CHUNK), 1)
    causal = col <= row
    n_chunks = x.shape[0] // CHUNK
    for g in range(GMLP_GROUPS):
        cols = slice(g * GMLP_GROUP_DIM, (g + 1) * GMLP_GROUP_DIM)
        u = jnp.dot(xn, win_ref[:, cols], preferred_element_type=F32)
        u = _gelu(u)
        ws = jnp.where(causal, ws_ref[g], 0.0).astype(BF16)
        bias = bs_ref[:, g:g + 1]
        for c in range(n_chunks):
            rows = slice(c * CHUNK, (c + 1) * CHUNK)
            mixed = jnp.dot(ws, vn_ref[rows, cols], preferred_element_type=F32) + bias
            gated_ref[rows, cols] = (u[rows, :] * mixed).astype(BF16)

    o_ref[...] = x + jnp.dot(gated_ref[...], wout_ref[...], preferred_element_type=F32)


def _gmlp_layer(x2d, norm_g, w_in, ln_g, ln_b, w_s, b_s_t, w_out):
    t = x2d.shape[0]
    tm = TM_GMLP
    row = lambda i: (i, 0)
    return pl.pallas_call(
        _gmlp_kernel,
        grid=(t // tm,),
        in_specs=[
            pl.BlockSpec((tm, D_MODEL), row),
            _resident((1, D_MODEL)),
            _resident((D_MODEL, 2 * GMLP_WIDTH)),
            _resident((1, GMLP_WIDTH)),
            _resident((1, GMLP_WIDTH)),
            _resident((GMLP_GROUPS, CHUNK, CHUNK)),
            _resident((CHUNK, GMLP_GROUPS)),
            _resident((GMLP_WIDTH, D_MODEL)),
        ],
        out_specs=pl.BlockSpec((tm, D_MODEL), row),
        out_shape=jax.ShapeDtypeStruct((t, D_MODEL), F32),
        scratch_shapes=[pltpu.VMEM((tm, GMLP_WIDTH), BF16), pltpu.VMEM((tm, GMLP_WIDTH), BF16)],
        compiler_params=_params(),
        name="gmlp",
    )(x2d, norm_g, w_in, ln_g, ln_b, w_s, b_s_t, w_out)


def _ffn_kernel(*refs, has_oproj, has_final_norm):
    refs = list(refs)
    h_ref = refs.pop(0)
    if has_oproj:
        a_ref = refs.pop(0)
        wo_ref = refs.pop(0)
    g_ref, wgu_ref, wd_ref = refs.pop(0), refs.pop(0), refs.pop(0)
    if has_final_norm:
        fg_ref = refs.pop(0)
    (o_ref,) = refs

    h = h_ref[...]
    if has_oproj:
        h = h + jnp.dot(a_ref[...], wo_ref[...], preferred_element_type=F32)
    xn = (_rms_scale(h) * g_ref[...]).astype(BF16)
    gate = jnp.dot(xn, wgu_ref[:, :D_FF], preferred_element_type=F32)
    up = jnp.dot(xn, wgu_ref[:, D_FF:], preferred_element_type=F32)
    act = (jax.nn.silu(gate) * up).astype(BF16)
    h = h + jnp.dot(act, wd_ref[...], preferred_element_type=F32)
    if has_final_norm:
        h = _rms_scale(h) * fg_ref[...]
    o_ref[...] = h


def _ffn_layer(h2d, norm_g, w_gu, w_down, attn=None, w_o=None, final_g=None):
    t = h2d.shape[0]
    tm = TM_FFN
    row = lambda i: (i, 0)
    has_oproj = attn is not None
    has_final_norm = final_g is not None
    args = [h2d]
    specs = [pl.BlockSpec((tm, D_MODEL), row)]
    if has_oproj:
        args += [attn, w_o]
        specs += [pl.BlockSpec((tm, D_V), row), _resident((D_V, D_MODEL))]
    args += [norm_g, w_gu, w_down]
    specs += [_resident((1, D_MODEL)), _resident((D_MODEL, 2 * D_FF)), _resident((D_FF, D_MODEL))]
    if has_final_norm:
        args.append(final_g)
        specs.append(_resident((1, D_MODEL)))
    return pl.pallas_call(
        functools.partial(_ffn_kernel, has_oproj=has_oproj, has_final_norm=has_final_norm),
        grid=(t // tm,),
        in_specs=specs,
        out_specs=pl.BlockSpec((tm, D_MODEL), row),
        out_shape=jax.ShapeDtypeStruct((t, D_MODEL), F32),
        compiler_params=_params(),
        name="ffn_final" if has_final_norm else "ffn",
    )(*args)


def _rope(x, ra, rb, rc):
    half = ROT_DIM // 2
    outs = []
    for j in range(x.shape[1] // LANES):
        xs = x[:, j * LANES:(j + 1) * LANES]
        fwd = pltpu.roll(xs, LANES - half, axis=1)
        bwd = pltpu.roll(xs, half, axis=1)
        outs.append(xs * ra + fwd * rb + bwd * rc)
    return outs


def _qkv_kernel(h_ref, gq_ref, gkv_ref, wq_ref, wkv_ref, ra_ref, rb_ref, rc_ref,
                qt_ref, k_ref, vt_ref):
    y = _rms_scale(h_ref[...])
    xq = (y * gq_ref[...]).astype(BF16)
    xkv = (y * gkv_ref[...]).astype(BF16)
    ra, rb, rc = ra_ref[...], rb_ref[...], rc_ref[...]
    q = jnp.dot(xq, wq_ref[...], preferred_element_type=F32)
    scale = DIFF_HEAD_DIM ** -0.5 * math.log2(math.e)
    for j, qs in enumerate(_rope(q, ra, rb, rc)):
        qt_ref[0, j * LANES:(j + 1) * LANES, :] = (qs * scale).T.astype(BF16)
    k = jnp.dot(xkv, wkv_ref[:, :D_K], preferred_element_type=F32)
    for j, ks in enumerate(_rope(k, ra, rb, rc)):
        k_ref[:, j * LANES:(j + 1) * LANES] = ks.astype(BF16)
    v = jnp.dot(xkv, wkv_ref[:, D_K:], preferred_element_type=F32)
    ones = jnp.ones((VT_ROWS - DIFF_V_DIM, TK), BF16)
    for t in range(v.shape[0] // TK):
        for j in range(DIFF_HEADS):
            vt_ref[0, t, j * VT_ROWS:j * VT_ROWS + DIFF_V_DIM, :] = (
                v[t * TK:(t + 1) * TK, j * DIFF_V_DIM:(j + 1) * DIFF_V_DIM].T.astype(BF16))
            vt_ref[0, t, j * VT_ROWS + DIFF_V_DIM:(j + 1) * VT_ROWS, :] = ones


def _rope_tables(seq):
    half = ROT_DIM // 2
    inv_freq = jnp.power(ROPE_THETA, -jnp.arange(half, dtype=F32) * 2.0 / ROT_DIM)
    ang = jnp.arange(seq, dtype=jnp.int32).astype(F32)[:, None] * inv_freq[None, :]
    cos, sin = jnp.cos(ang), jnp.sin(ang)
    ones = jnp.ones((seq, DIFF_HEAD_DIM - ROT_DIM), F32)
    zeros_h = jnp.zeros((seq, half), F32)
    zeros_r = jnp.zeros((seq, DIFF_HEAD_DIM - ROT_DIM), F32)
    ra = jnp.concatenate([cos, cos, ones], axis=1)
    rb = jnp.concatenate([-sin, zeros_h, zeros_r], axis=1)
    rc = jnp.concatenate([zeros_h, sin, zeros_r], axis=1)
    rep = LANES // DIFF_HEAD_DIM
    return tuple(jnp.tile(a, (1, rep)) for a in (ra, rb, rc))


def _qkv_proj(h2d, batch, seq, gq, gkv, w_q, w_kv):
    tm = TM_QKV
    assert tm % TK == 0
    nb = seq // tm
    row = lambda b, i: (b * nb + i, 0)
    pos = lambda b, i: (i, 0)
    col = lambda b, i: (b, 0, i)
    ra, rb, rc = _rope_tables(seq)
    return pl.pallas_call(
        _qkv_kernel,
        grid=(batch, nb),
        in_specs=[
            pl.BlockSpec((tm, D_MODEL), row),
            _resident((1, D_MODEL)),
            _resident((1, D_MODEL)),
            _resident((D_MODEL, D_K)),
            _resident((D_MODEL, D_K + D_V)),
            pl.BlockSpec((tm, LANES), pos),
            pl.BlockSpec((tm, LANES), pos),
            pl.BlockSpec((tm, LANES), pos),
        ],
        out_specs=[pl.BlockSpec((1, D_K, tm), col), pl.BlockSpec((tm, D_K), row),
                   pl.BlockSpec((1, tm // TK, DIFF_HEADS * VT_ROWS, TK), lambda b, i: (b, i, 0, 0))],
        out_shape=[jax.ShapeDtypeStruct((batch, D_K, seq), BF16),
                   jax.ShapeDtypeStruct((batch * seq, D_K), BF16),
                   jax.ShapeDtypeStruct((batch, seq // TK, DIFF_HEADS * VT_ROWS, TK), BF16)],
        compiler_params=_params(),
        name="qkv",
    )(h2d, gq, gkv, w_q, w_kv, ra, rb, rc)


def _attn_kernel(qt_ref, k_ref, vt_ref, lq_ref, lk_ref, subg_ref, o_ref, qq_ref, acc_ref, s_ref, *,
                 lam_init):
    i = pl.program_id(2)
    n_half = TQ // TC
    d = DIFF_HEAD_DIM
    dv = DIFF_V_DIM

    all_chains = list(range(2 * n_half))

    zero = jnp.zeros((d, TC), BF16)
    for r in range(n_half):
        qt = qt_ref[0, :, r * TC:(r + 1) * TC]
        qq_ref[r, :d, :] = qt[:d, :]
        qq_ref[r, d:, :] = zero
        qq_ref[n_half + r, :d, :] = zero
        qq_ref[n_half + r, d:, :] = qt[d:, :]
    acc_ref[...] = jnp.zeros(acc_ref.shape, F32)

    key_idx = lax.broadcasted_iota(jnp.int32, (TK, TC), 0)
    qry_idx = lax.broadcasted_iota(jnp.int32, (TK, TC), 1)
    causal = key_idx <= qry_idx

    def chains_from(first_chain_row):
        return [c for c in all_chains if first_chain_row is None or c % n_half >= first_chain_row]

    def scores(j, slot, first_chain_row):
        kj = k_ref[pl.ds(pl.multiple_of(j * TK, TK), TK), :]
        col_max = [None] * len(all_chains)
        for c in chains_from(first_chain_row):
            s = jnp.dot(kj, qq_ref[c], preferred_element_type=F32)
            s_ref[slot, c] = s
            col_max[c] = jnp.max(s, axis=0, keepdims=True)
        return tuple(col_max)

    def softmax_pv(j, slot, ms, col_max, first_chain_row):
        vtj = vt_ref[0, j]
        ms = list(ms)
        for c in chains_from(first_chain_row):
            s = s_ref[slot, c]
            mx = col_max[c]
            if first_chain_row is not None and c % n_half == first_chain_row:
                s = jnp.where(causal, s, -jnp.inf)
                mx = jnp.max(s, axis=0, keepdims=True)
            m_new = jnp.maximum(ms[c], mx)
            alpha = jnp.exp2(ms[c] - m_new)
            p = jnp.exp2(s - m_new).astype(BF16)
            ms[c] = m_new
            pv = jnp.dot(vtj, p, preferred_element_type=F32)
            acc_ref[c] = alpha * acc_ref[c] + pv
        return tuple(ms)

    def earlier_query_block(qb, carry):
        ms, ahead = carry
        ahead = list(ahead)
        b0 = qb * SCORE_SLOTS
        for u in range(SCORE_SLOTS):
            ahead.append(scores(b0 + u + LOOKAHEAD, (u + LOOKAHEAD) % SCORE_SLOTS, None))
            ms = softmax_pv(b0 + u, u, ms, ahead.pop(0), None)
        return ms, tuple(ahead)

    first_diag = i * SCORE_SLOTS
    ms = tuple(jnp.full((1, TC), -jnp.inf, F32) for _ in all_chains)
    ahead = tuple(scores(u, u, None) for u in range(LOOKAHEAD))
    ms, ahead = lax.fori_loop(0, i, earlier_query_block, (ms, ahead))
    ahead = list(ahead)
    for r in range(SCORE_SLOTS):
        if r + LOOKAHEAD < SCORE_SLOTS:
            ahead.append(scores(first_diag + r + LOOKAHEAD, r + LOOKAHEAD, r + LOOKAHEAD))
        ms = softmax_pv(first_diag + r, r, ms, ahead.pop(0), r)

    prod = lq_ref[...] * lk_ref[...]
    e = jnp.exp(jnp.sum(prod, axis=-1, keepdims=True))
    lam = e[0:1, :] - e[1:2, :] + lam_init
    for r in range(n_half):
        c1, c2 = r, n_half + r
        o1 = acc_ref[c1, :dv, :] * (1.0 / acc_ref[c1, dv:dv + 1, :])
        o2 = acc_ref[c2, :dv, :] * (1.0 / acc_ref[c2, dv:dv + 1, :])
        o = (o1 - lam * o2).T
        o = _rms_scale(o) * subg_ref[...]
        o_ref[r * TC:(r + 1) * TC, :] = (o * (1.0 - lam_init)).astype(BF16)


def _diff_attention(qt, k, vt, batch, seq, lam_q, lam_k, sub_g, lam_init):
    assert TC == TK and TQ == SCORE_SLOTS * TK and 0 < LOOKAHEAD < SCORE_SLOTS
    nq = seq // TQ
    return pl.pallas_call(
        functools.partial(_attn_kernel, lam_init=lam_init),
        grid=(batch, DIFF_HEADS, nq),
        in_specs=[
            pl.BlockSpec((1, LANES, TQ), lambda b, h, i: (b, h, i)),
            pl.BlockSpec((seq, LANES), lambda b, h, i: (b, h)),
            pl.BlockSpec((1, seq // TK, VT_ROWS, TK), lambda b, h, i: (b, 0, h, 0)),
            _resident((2, DIFF_HEAD_DIM)),
            _resident((2, DIFF_HEAD_DIM)),
            _resident((1, DIFF_V_DIM)),
        ],
        out_specs=pl.BlockSpec((TQ, LANES), lambda b, h, i: (b * nq + i, h)),
        out_shape=jax.ShapeDtypeStruct((batch * seq, D_V), BF16),
        scratch_shapes=[pltpu.VMEM((2 * TQ // TC, LANES, TC), BF16),
                        pltpu.VMEM((2 * TQ // TC, VT_ROWS, TC), F32),
                        pltpu.VMEM((SCORE_SLOTS, 2 * TQ // TC, TK, TC), F32)],
        compiler_params=_params(),
        name="diff_attn",
    )(qt, k, vt, lam_q, lam_k, sub_g)


def kernel(x, attn_norm_g, ffn_norm_g, gmlp_w_in, gmlp_ln_g, gmlp_ln_b, gmlp_w_s, gmlp_b_s, gmlp_w_out, kv_norm_g, w_kv, diff_w_q, diff_lambda_q, diff_lambda_k, diff_sub_g, diff_w_o, ffn_w_gu, ffn_w_down, final_norm_g):
    batch, seq, _ = x.shape
    depth = attn_norm_g.shape[0]
    assert depth == 2 and gmlp_w_in.shape[0] == N_A and diff_w_q.shape[0] == depth - N_A
    row = lambda a: a.reshape(1, -1)
    h = x.reshape(batch * seq, D_MODEL)

    h = _gmlp_layer(h, row(attn_norm_g[0]), gmlp_w_in[0].astype(BF16), row(gmlp_ln_g[0]),
                    row(gmlp_ln_b[0]), gmlp_w_s[0], gmlp_b_s[0].T, gmlp_w_out[0].astype(BF16))
    h = _ffn_layer(h, row(ffn_norm_g[0]), ffn_w_gu[0].astype(BF16), ffn_w_down[0].astype(BF16))

    qt, k, vt = _qkv_proj(h, batch, seq, row(attn_norm_g[1]), row(kv_norm_g),
                          diff_w_q[0].astype(BF16), w_kv.astype(BF16))
    lam_init = 0.8 - 0.6 * math.exp(-0.3 * 1)
    attn = _diff_attention(qt, k, vt, batch, seq, diff_lambda_q[0], diff_lambda_k[0],
                           row(diff_sub_g[0]), lam_init)
    h = _ffn_layer(h, row(ffn_norm_g[1]), ffn_w_gu[1].astype(BF16), ffn_w_down[1].astype(BF16),
                   attn=attn, w_o=diff_w_o[0].astype(BF16), final_g=row(final_norm_g))
    return h.reshape(batch, seq, D_MODEL)
```

```python
import functools
import math

import jax
import jax.numpy as jnp
from jax import lax
from jax.experimental import pallas as pl
from jax.experimental.pallas import tpu as pltpu

D_MODEL = 1024
CHUNK = 128
GMLP_WIDTH = 2 * D_MODEL
GMLP_GROUPS = 8
GMLP_GROUP_DIM = GMLP_WIDTH // GMLP_GROUPS
DIFF_HEADS = 8
DIFF_HEAD_DIM = D_MODEL // (2 * DIFF_HEADS)
DIFF_V_DIM = 2 * DIFF_HEAD_DIM
D_K = DIFF_HEADS * 2 * DIFF_HEAD_DIM
D_V = DIFF_HEADS * DIFF_V_DIM
ROT_DIM = DIFF_HEAD_DIM // 4
ROPE_THETA = 500000.0
D_FF = -(-(8 * D_MODEL) // (3 * 256)) * 256
EPS = 1e-5
N_A = 1

LANES = 128
VMEM_LIMIT_BYTES = 56 * 2**20

TM_GMLP = 512
TM_FFN = 512
TM_QKV = 1024
TQ = 1024
TK = 512
TC = 256
SCORE_SLOTS = TQ // TK
LOOKAHEAD = 1
BF16_SUBLANES = 16
VT_ROWS = DIFF_V_DIM + BF16_SUBLANES

F32 = jnp.float32
BF16 = jnp.bfloat16


def _rms_scale(x):
    return x * lax.rsqrt(jnp.mean(x * x, axis=-1, keepdims=True) + EPS)


def _gelu(x):
    return 0.5 * x * (1.0 + lax.erf(x * math.sqrt(0.5)))


def _resident(shape):
    return pl.BlockSpec(shape, lambda *_: (0,) * len(shape), pipeline_mode=pl.Buffered(1))


def _params():
    return pltpu.CompilerParams(vmem_limit_bytes=VMEM_LIMIT_BYTES)


def _gmlp_kernel(x_ref, g_ref, win_ref, lng_ref, lnb_ref, ws_ref, bs_ref, wout_ref, o_ref,
                 vn_ref, gated_ref):
    x = x_ref[...]
    xn = (_rms_scale(x) * g_ref[...]).astype(BF16)

    v = jnp.dot(xn, win_ref[:, GMLP_WIDTH:], preferred_element_type=F32)
    v = _gelu(v)
    mu = jnp.mean(v, axis=-1, keepdims=True)
    vc = v - mu
    vn = vc * lax.rsqrt(jnp.mean(vc * vc, axis=-1, keepdims=True) + EPS)
    vn_ref[...] = (vn * lng_ref[...] + lnb_ref[...]).astype(BF16)

    row = lax.broadcasted_iota(jnp.int32, (CHUNK, CHUNK), 0)
    col = lax.broadcasted_iota(jnp.int32, (CHUNK, CHUNK), 1)
    causal = col <= row
    n_chunks = x.shape[0] // CHUNK
    for g in range(GMLP_GROUPS):
        cols = slice(g * GMLP_GROUP_DIM, (g + 1) * GMLP_GROUP_DIM)
        u = jnp.dot(xn, win_ref[:, cols], preferred_element_type=F32)
        u = _gelu(u)
        ws = jnp.where(causal, ws_ref[g], 0.0).astype(BF16)
        bias = bs_ref[:, g:g + 1]
        for c in range(n_chunks):
            rows = slice(c * CHUNK, (c + 1) * CHUNK)
            mixed = jnp.dot(ws, vn_ref[rows, cols], preferred_element_type=F32) + bias
            gated_ref[rows, cols] = (u[rows, :] * mixed).astype(BF16)

    o_ref[...] = x + jnp.dot(gated_ref[...], wout_ref[...], preferred_element_type=F32)


def _gmlp_layer(x2d, norm_g, w_in, ln_g, ln_b, w_s, b_s_t, w_out):
    t = x2d.shape[0]
    tm = TM_GMLP
    row = lambda i: (i, 0)
    return pl.pallas_call(
        _gmlp_kernel,
        grid=(t // tm,),
        in_specs=[
            pl.BlockSpec((tm, D_MODEL), row),
            _resident((1, D_MODEL)),
            _resident((D_MODEL, 2 * GMLP_WIDTH)),
            _resident((1, GMLP_WIDTH)),
            _resident((1, GMLP_WIDTH)),
            _resident((GMLP_GROUPS, CHUNK, CHUNK)),
            _resident((CHUNK, GMLP_GROUPS)),
            _resident((GMLP_WIDTH, D_MODEL)),
        ],
        out_specs=pl.BlockSpec((tm, D_MODEL), row),
        out_shape=jax.ShapeDtypeStruct((t, D_MODEL), F32),
        scratch_shapes=[pltpu.VMEM((tm, GMLP_WIDTH), BF16), pltpu.VMEM((tm, GMLP_WIDTH), BF16)],
        compiler_params=_params(),
        name="gmlp",
    )(x2d, norm_g, w_in, ln_g, ln_b, w_s, b_s_t, w_out)


def _ffn_kernel(*refs, has_oproj, has_final_norm):
    refs = list(refs)
    h_ref = refs.pop(0)
    if has_oproj:
        a_ref = refs.pop(0)
        wo_ref = refs.pop(0)
    g_ref, wgu_ref, wd_ref = refs.pop(0), refs.pop(0), refs.pop(0)
    if has_final_norm:
        fg_ref = refs.pop(0)
    (o_ref,) = refs

    h = h_ref[...]
    if has_oproj:
        h = h + jnp.dot(a_ref[...], wo_ref[...], preferred_element_type=F32)
    xn = (_rms_scale(h) * g_ref[...]).astype(BF16)
    gate = jnp.dot(xn, wgu_ref[:, :D_FF], preferred_element_type=F32)
    up = jnp.dot(xn, wgu_ref[:, D_FF:], preferred_element_type=F32)
    act = (jax.nn.silu(gate) * up).astype(BF16)
    h = h + jnp.dot(act, wd_ref[...], preferred_element_type=F32)
    if has_final_norm:
        h = _rms_scale(h) * fg_ref[...]
    o_ref[...] = h


def _ffn_layer(h2d, norm_g, w_gu, w_down, attn=None, w_o=None, final_g=None):
    t = h2d.shape[0]
    tm = TM_FFN
    row = lambda i: (i, 0)
    has_oproj = attn is not None
    has_final_norm = final_g is not None
    args = [h2d]
    specs = [pl.BlockSpec((tm, D_MODEL), row)]
    if has_oproj:
        args += [attn, w_o]
        specs += [pl.BlockSpec((tm, D_V), row), _resident((D_V, D_MODEL))]
    args += [norm_g, w_gu, w_down]
    specs += [_resident((1, D_MODEL)), _resident((D_MODEL, 2 * D_FF)), _resident((D_FF, D_MODEL))]
    if has_final_norm:
        args.append(final_g)
        specs.append(_resident((1, D_MODEL)))
    return pl.pallas_call(
        functools.partial(_ffn_kernel, has_oproj=has_oproj, has_final_norm=has_final_norm),
        grid=(t // tm,),
        in_specs=specs,
        out_specs=pl.BlockSpec((tm, D_MODEL), row),
        out_shape=jax.ShapeDtypeStruct((t, D_MODEL), F32),
        compiler_params=_params(),
        name="ffn_final" if has_final_norm else "ffn",
    )(*args)


def _rope(x, ra, rb, rc):
    half = ROT_DIM // 2
    outs = []
    for j in range(x.shape[1] // LANES):
        xs = x[:, j * LANES:(j + 1) * LANES]
        fwd = pltpu.roll(xs, LANES - half, axis=1)
        bwd = pltpu.roll(xs, half, axis=1)
        outs.append(xs * ra + fwd * rb + bwd * rc)
    return outs


def _qkv_kernel(h_ref, gq_ref, gkv_ref, wq_ref, wkv_ref, ra_ref, rb_ref, rc_ref,
                qt_ref, k_ref, vt_ref):
    y = _rms_scale(h_ref[...])
    xq = (y * gq_ref[...]).astype(BF16)
    xkv = (y * gkv_ref[...]).astype(BF16)
    ra, rb, rc = ra_ref[...], rb_ref[...], rc_ref[...]
    q = jnp.dot(xq, wq_ref[...], preferred_element_type=F32)
    scale = DIFF_HEAD_DIM ** -0.5 * math.log2(math.e)
    for j, qs in enumerate(_rope(q, ra, rb, rc)):
        qt_ref[0, j * LANES:(j + 1) * LANES, :] = (qs * scale).T.astype(BF16)
    k = jnp.dot(xkv, wkv_ref[:, :D_K], preferred_element_type=F32)
    for j, ks in enumerate(_rope(k, ra, rb, rc)):
        k_ref[:, j * LANES:(j + 1) * LANES] = ks.astype(BF16)
    v = jnp.dot(xkv, wkv_ref[:, D_K:], preferred_element_type=F32)
    ones = jnp.ones((VT_ROWS - DIFF_V_DIM, TK), BF16)
    for t in range(v.shape[0] // TK):
        for j in range(DIFF_HEADS):
            vt_ref[0, t, j * VT_ROWS:j * VT_ROWS + DIFF_V_DIM, :] = (
                v[t * TK:(t + 1) * TK, j * DIFF_V_DIM:(j + 1) * DIFF_V_DIM].T.astype(BF16))
            vt_ref[0, t, j * VT_ROWS + DIFF_V_DIM:(j + 1) * VT_ROWS, :] = ones


def _rope_tables(seq):
    half = ROT_DIM // 2
    inv_freq = jnp.power(ROPE_THETA, -jnp.arange(half, dtype=F32) * 2.0 / ROT_DIM)
    ang = jnp.arange(seq, dtype=jnp.int32).astype(F32)[:, None] * inv_freq[None, :]
    cos, sin = jnp.cos(ang), jnp.sin(ang)
    ones = jnp.ones((seq, DIFF_HEAD_DIM - ROT_DIM), F32)
    zeros_h = jnp.zeros((seq, half), F32)
    zeros_r = jnp.zeros((seq, DIFF_HEAD_DIM - ROT_DIM), F32)
    ra = jnp.concatenate([cos, cos, ones], axis=1)
    rb = jnp.concatenate([-sin, zeros_h, zeros_r], axis=1)
    rc = jnp.concatenate([zeros_h, sin, zeros_r], axis=1)
    rep = LANES // DIFF_HEAD_DIM
    return tuple(jnp.tile(a, (1, rep)) for a in (ra, rb, rc))


def _qkv_proj(h2d, batch, seq, gq, gkv, w_q, w_kv):
    tm = TM_QKV
    assert tm % TK == 0
    nb = seq // tm
    row = lambda b, i: (b * nb + i, 0)
    pos = lambda b, i: (i, 0)
    col = lambda b, i: (b, 0, i)
    ra, rb, rc = _rope_tables(seq)
    return pl.pallas_call(
        _qkv_kernel,
        grid=(batch, nb),
        in_specs=[
            pl.BlockSpec((tm, D_MODEL), row),
            _resident((1, D_MODEL)),
            _resident((1, D_MODEL)),
            _resident((D_MODEL, D_K)),
            _resident((D_MODEL, D_K + D_V)),
            pl.BlockSpec((tm, LANES), pos),
            pl.BlockSpec((tm, LANES), pos),
            pl.BlockSpec((tm, LANES), pos),
        ],
        out_specs=[pl.BlockSpec((1, D_K, tm), col), pl.BlockSpec((tm, D_K), row),
                   pl.BlockSpec((1, tm // TK, DIFF_HEADS * VT_ROWS, TK), lambda b, i: (b, i, 0, 0))],
        out_shape=[jax.ShapeDtypeStruct((batch, D_K, seq), BF16),
                   jax.ShapeDtypeStruct((batch * seq, D_K), BF16),
                   jax.ShapeDtypeStruct((batch, seq // TK, DIFF_HEADS * VT_ROWS, TK), BF16)],
        compiler_params=_params(),
        name="qkv",
    )(h2d, gq, gkv, w_q, w_kv, ra, rb, rc)


def _attn_kernel(qt_ref, k_ref, vt_ref, lq_ref, lk_ref, subg_ref, o_ref, qq_ref, acc_ref, s_ref, *,
                 lam_init):
    i = pl.program_id(2)
    n_half = TQ // TC
    d = DIFF_HEAD_DIM
    dv = DIFF_V_DIM

    all_chains = list(range(2 * n_half))

    zero = jnp.zeros((d, TC), BF16)
    for r in range(n_half):
        qt = qt_ref[0, :, r * TC:(r + 1) * TC]
        qq_ref[r, :d, :] = qt[:d, :]
        qq_ref[r, d:, :] = zero
        qq_ref[n_half + r, :d, :] = zero
        qq_ref[n_half + r, d:, :] = qt[d:, :]
    acc_ref[...] = jnp.zeros(acc_ref.shape, F32)

    key_idx = lax.broadcasted_iota(jnp.int32, (TK, TC), 0)
    qry_idx = lax.broadcasted_iota(jnp.int32, (TK, TC), 1)

    def visibility(c, diag_block):
        if diag_block is None:
            return 'all'
        first_query, first_key = (c % n_half) * TC, diag_block * TK
        if first_query + TC - 1 < first_key:
            return 'none'
        if first_query >= first_key + TK - 1:
            return 'all'
        return 'partial'

    def chains_from(diag_block):
        return [c for c in all_chains if visibility(c, diag_block) != 'none']

    def scores(j, slot, first_chain_row):
        kj = k_ref[pl.ds(pl.multiple_of(j * TK, TK), TK), :]
        col_max = [None] * len(all_chains)
        for c in chains_from(first_chain_row):
            s = jnp.dot(kj, qq_ref[c], preferred_element_type=F32)
            s_ref[slot, c] = s
            col_max[c] = jnp.max(s, axis=0, keepdims=True)
        return tuple(col_max)

    def softmax_pv(j, slot, ms, col_max, first_chain_row):
        vtj = vt_ref[0, j]
        ms = list(ms)
        for c in chains_from(first_chain_row):
            s = s_ref[slot, c]
            mx = col_max[c]
            if visibility(c, first_chain_row) == 'partial':
                key_minus_query = first_chain_row * TK - (c % n_half) * TC
                s = jnp.where(key_idx + key_minus_query <= qry_idx, s, -jnp.inf)
                mx = jnp.max(s, axis=0, keepdims=True)
            m_new = jnp.maximum(ms[c], mx)
            alpha = jnp.exp2(ms[c] - m_new)
            p = jnp.exp2(s - m_new).astype(BF16)
            ms[c] = m_new
            pv = jnp.dot(vtj, p, preferred_element_type=F32)
            acc_ref[c] = alpha * acc_ref[c] + pv
        return tuple(ms)

    def earlier_query_block(qb, carry):
        ms, ahead = carry
        ahead = list(ahead)
        b0 = qb * SCORE_SLOTS
        for u in range(SCORE_SLOTS):
            ahead.append(scores(b0 + u + LOOKAHEAD, (u + LOOKAHEAD) % SCORE_SLOTS, None))
            ms = softmax_pv(b0 + u, u, ms, ahead.pop(0), None)
        return ms, tuple(ahead)

    first_diag = i * SCORE_SLOTS
    ms = tuple(jnp.full((1, TC), -jnp.inf, F32) for _ in all_chains)
    ahead = tuple(scores(u, u, None) for u in range(LOOKAHEAD))
    ms, ahead = lax.fori_loop(0, i, earlier_query_block, (ms, ahead))
    ahead = list(ahead)
    for r in range(SCORE_SLOTS):
        if r + LOOKAHEAD < SCORE_SLOTS:
            ahead.append(scores(first_diag + r + LOOKAHEAD, r + LOOKAHEAD, r + LOOKAHEAD))
        ms = softmax_pv(first_diag + r, r, ms, ahead.pop(0), r)

    prod = lq_ref[...] * lk_ref[...]
    e = jnp.exp(jnp.sum(prod, axis=-1, keepdims=True))
    lam = e[0:1, :] - e[1:2, :] + lam_init
    for r in range(n_half):
        c1, c2 = r, n_half + r
        o1 = acc_ref[c1, :dv, :] * (1.0 / acc_ref[c1, dv:dv + 1, :])
        o2 = acc_ref[c2, :dv, :] * (1.0 / acc_ref[c2, dv:dv + 1, :])
        o = (o1 - lam * o2).T
        o = _rms_scale(o) * subg_ref[...]
        o_ref[r * TC:(r + 1) * TC, :] = (o * (1.0 - lam_init)).astype(BF16)


def _diff_attention(qt, k, vt, batch, seq, lam_q, lam_k, sub_g, lam_init):
    assert TQ == SCORE_SLOTS * TK and TK % TC == 0 and 0 < LOOKAHEAD < SCORE_SLOTS
    nq = seq // TQ
    return pl.pallas_call(
        functools.partial(_attn_kernel, lam_init=lam_init),
        grid=(batch, DIFF_HEADS, nq),
        in_specs=[
            pl.BlockSpec((1, LANES, TQ), lambda b, h, i: (b, h, i)),
            pl.BlockSpec((seq, LANES), lambda b, h, i: (b, h)),
            pl.BlockSpec((1, seq // TK, VT_ROWS, TK), lambda b, h, i: (b, 0, h, 0)),
            _resident((2, DIFF_HEAD_DIM)),
            _resident((2, DIFF_HEAD_DIM)),
            _resident((1, DIFF_V_DIM)),
        ],
        out_specs=pl.BlockSpec((TQ, LANES), lambda b, h, i: (b * nq + i, h)),
        out_shape=jax.ShapeDtypeStruct((batch * seq, D_V), BF16),
        scratch_shapes=[pltpu.VMEM((2 * TQ // TC, LANES, TC), BF16),
                        pltpu.VMEM((2 * TQ // TC, VT_ROWS, TC), F32),
                        pltpu.VMEM((SCORE_SLOTS, 2 * TQ // TC, TK, TC), F32)],
        compiler_params=_params(),
        name="diff_attn",
    )(qt, k, vt, lam_q, lam_k, sub_g)


def kernel(x, attn_norm_g, ffn_norm_g, gmlp_w_in, gmlp_ln_g, gmlp_ln_b, gmlp_w_s, gmlp_b_s, gmlp_w_out, kv_norm_g, w_kv, diff_w_q, diff_lambda_q, diff_lambda_k, diff_sub_g, diff_w_o, ffn_w_gu, ffn_w_down, final_norm_g):
    batch, seq, _ = x.shape
    depth = attn_norm_g.shape[0]
    assert depth == 2 and gmlp_w_in.shape[0] == N_A and diff_w_q.shape[0] == depth - N_A
    row = lambda a: a.reshape(1, -1)
    h = x.reshape(batch * seq, D_MODEL)

    h = _gmlp_layer(h, row(attn_norm_g[0]), gmlp_w_in[0].astype(BF16), row(gmlp_ln_g[0]),
                    row(gmlp_ln_b[0]), gmlp_w_s[0], gmlp_b_s[0].T, gmlp_w_out[0].astype(BF16))
    h = _ffn_layer(h, row(ffn_norm_g[0]), ffn_w_gu[0].astype(BF16), ffn_w_down[0].astype(BF16))

    qt, k, vt = _qkv_proj(h, batch, seq, row(attn_norm_g[1]), row(kv_norm_g),
                          diff_w_q[0].astype(BF16), w_kv.astype(BF16))
    lam_init = 0.8 - 0.6 * math.exp(-0.3 * 1)
    attn = _diff_attention(qt, k, vt, batch, seq, diff_lambda_q[0], diff_lambda_k[0],
                           row(diff_sub_g[0]), lam_init)
    h = _ffn_layer(h, row(ffn_norm_g[1]), ffn_w_gu[1].astype(BF16), ffn_w_down[1].astype(BF16),
                   attn=attn, w_o=diff_w_o[0].astype(BF16), final_g=row(final_norm_g))
    return h.reshape(batch, seq, D_MODEL)
```

```python
import functools
import math

import jax
import jax.numpy as jnp
from jax import lax
from jax.experimental import pallas as pl
from jax.experimental.pallas import tpu as pltpu

D_MODEL = 1024
CHUNK = 128
GMLP_WIDTH = 2 * D_MODEL
GMLP_GROUPS = 8
GMLP_GROUP_DIM = GMLP_WIDTH // GMLP_GROUPS
DIFF_HEADS = 8
DIFF_HEAD_DIM = D_MODEL // (2 * DIFF_HEADS)
DIFF_V_DIM = 2 * DIFF_HEAD_DIM
D_K = DIFF_HEADS * 2 * DIFF_HEAD_DIM
D_V = DIFF_HEADS * DIFF_V_DIM
ROT_DIM = DIFF_HEAD_DIM // 4
ROPE_THETA = 500000.0
D_FF = -(-(8 * D_MODEL) // (3 * 256)) * 256
EPS = 1e-5
N_A = 1

LANES = 128
VMEM_LIMIT_BYTES = 56 * 2**20

TM_GMLP = 512
TM_FFN = 512
TM_QKV = 1024
TQ = 1024
TK = 256
TC = 256
SCORE_SLOTS = TQ // TK
LOOKAHEAD = 2
BF16_SUBLANES = 16
F32_SUBLANES = 8
VT_ROWS = DIFF_V_DIM + BF16_SUBLANES

F32 = jnp.float32
BF16 = jnp.bfloat16


def _rms_scale(x):
    return x * lax.rsqrt(jnp.mean(x * x, axis=-1, keepdims=True) + EPS)


def _gelu(x):
    return 0.5 * x * (1.0 + lax.erf(x * math.sqrt(0.5)))


def _resident(shape):
    return pl.BlockSpec(shape, lambda *_: (0,) * len(shape), pipeline_mode=pl.Buffered(1))


def _params():
    return pltpu.CompilerParams(vmem_limit_bytes=VMEM_LIMIT_BYTES)


def _gmlp_kernel(x_ref, g_ref, win_ref, lng_ref, lnb_ref, ws_ref, bs_ref, wout_ref, o_ref,
                 vn_ref, gated_ref):
    x = x_ref[...]
    xn = (_rms_scale(x) * g_ref[...]).astype(BF16)

    v = jnp.dot(xn, win_ref[:, GMLP_WIDTH:], preferred_element_type=F32)
    v = _gelu(v)
    mu = jnp.mean(v, axis=-1, keepdims=True)
    vc = v - mu
    vn = vc * lax.rsqrt(jnp.mean(vc * vc, axis=-1, keepdims=True) + EPS)
    vn_ref[...] = (vn * lng_ref[...] + lnb_ref[...]).astype(BF16)

    row = lax.broadcasted_iota(jnp.int32, (CHUNK, CHUNK), 0)
    col = lax.broadcasted_iota(jnp.int32, (CHUNK, CHUNK), 1)
    causal = col <= row
    n_chunks = x.shape[0] // CHUNK
    for g in range(GMLP_GROUPS):
        cols = slice(g * GMLP_GROUP_DIM, (g + 1) * GMLP_GROUP_DIM)
        u = jnp.dot(xn, win_ref[:, cols], preferred_element_type=F32)
        u = _gelu(u)
        ws = jnp.where(causal, ws_ref[g], 0.0).astype(BF16)
        bias = bs_ref[:, g:g + 1]
        for c in range(n_chunks):
            rows = slice(c * CHUNK, (c + 1) * CHUNK)
            mixed = jnp.dot(ws, vn_ref[rows, cols], preferred_element_type=F32) + bias
            gated_ref[rows, cols] = (u[rows, :] * mixed).astype(BF16)

    o_ref[...] = x + jnp.dot(gated_ref[...], wout_ref[...], preferred_element_type=F32)


def _gmlp_layer(x2d, norm_g, w_in, ln_g, ln_b, w_s, b_s_t, w_out):
    t = x2d.shape[0]
    tm = TM_GMLP
    row = lambda i: (i, 0)
    return pl.pallas_call(
        _gmlp_kernel,
        grid=(t // tm,),
        in_specs=[
            pl.BlockSpec((tm, D_MODEL), row),
            _resident((1, D_MODEL)),
            _resident((D_MODEL, 2 * GMLP_WIDTH)),
            _resident((1, GMLP_WIDTH)),
            _resident((1, GMLP_WIDTH)),
            _resident((GMLP_GROUPS, CHUNK, CHUNK)),
            _resident((CHUNK, GMLP_GROUPS)),
            _resident((GMLP_WIDTH, D_MODEL)),
        ],
        out_specs=pl.BlockSpec((tm, D_MODEL), row),
        out_shape=jax.ShapeDtypeStruct((t, D_MODEL), F32),
        scratch_shapes=[pltpu.VMEM((tm, GMLP_WIDTH), BF16), pltpu.VMEM((tm, GMLP_WIDTH), BF16)],
        compiler_params=_params(),
        name="gmlp",
    )(x2d, norm_g, w_in, ln_g, ln_b, w_s, b_s_t, w_out)


def _ffn_kernel(*refs, has_oproj, has_final_norm):
    refs = list(refs)
    h_ref = refs.pop(0)
    if has_oproj:
        a_ref = refs.pop(0)
        wo_ref = refs.pop(0)
    g_ref, wgu_ref, wd_ref = refs.pop(0), refs.pop(0), refs.pop(0)
    if has_final_norm:
        fg_ref = refs.pop(0)
    (o_ref,) = refs

    h = h_ref[...]
    if has_oproj:
        h = h + jnp.dot(a_ref[...], wo_ref[...], preferred_element_type=F32)
    xn = (_rms_scale(h) * g_ref[...]).astype(BF16)
    gate = jnp.dot(xn, wgu_ref[:, :D_FF], preferred_element_type=F32)
    up = jnp.dot(xn, wgu_ref[:, D_FF:], preferred_element_type=F32)
    act = (jax.nn.silu(gate) * up).astype(BF16)
    h = h + jnp.dot(act, wd_ref[...], preferred_element_type=F32)
    if has_final_norm:
        h = _rms_scale(h) * fg_ref[...]
    o_ref[...] = h


def _ffn_layer(h2d, norm_g, w_gu, w_down, attn=None, w_o=None, final_g=None):
    t = h2d.shape[0]
    tm = TM_FFN
    row = lambda i: (i, 0)
    has_oproj = attn is not None
    has_final_norm = final_g is not None
    args = [h2d]
    specs = [pl.BlockSpec((tm, D_MODEL), row)]
    if has_oproj:
        args += [attn, w_o]
        specs += [pl.BlockSpec((tm, D_V), row), _resident((D_V, D_MODEL))]
    args += [norm_g, w_gu, w_down]
    specs += [_resident((1, D_MODEL)), _resident((D_MODEL, 2 * D_FF)), _resident((D_FF, D_MODEL))]
    if has_final_norm:
        args.append(final_g)
        specs.append(_resident((1, D_MODEL)))
    return pl.pallas_call(
        functools.partial(_ffn_kernel, has_oproj=has_oproj, has_final_norm=has_final_norm),
        grid=(t // tm,),
        in_specs=specs,
        out_specs=pl.BlockSpec((tm, D_MODEL), row),
        out_shape=jax.ShapeDtypeStruct((t, D_MODEL), F32),
        compiler_params=_params(),
        name="ffn_final" if has_final_norm else "ffn",
    )(*args)


def _rope(x, ra, rb, rc):
    half = ROT_DIM // 2
    outs = []
    for j in range(x.shape[1] // LANES):
        xs = x[:, j * LANES:(j + 1) * LANES]
        fwd = pltpu.roll(xs, LANES - half, axis=1)
        bwd = pltpu.roll(xs, half, axis=1)
        outs.append(xs * ra + fwd * rb + bwd * rc)
    return outs


def _qkv_kernel(h_ref, gq_ref, gkv_ref, wq_ref, wkv_ref, ra_ref, rb_ref, rc_ref,
                qt_ref, k_ref, vt_ref):
    y = _rms_scale(h_ref[...])
    xq = (y * gq_ref[...]).astype(BF16)
    xkv = (y * gkv_ref[...]).astype(BF16)
    ra, rb, rc = ra_ref[...], rb_ref[...], rc_ref[...]
    q = jnp.dot(xq, wq_ref[...], preferred_element_type=F32)
    scale = DIFF_HEAD_DIM ** -0.5 * math.log2(math.e)
    for j, qs in enumerate(_rope(q, ra, rb, rc)):
        qt_ref[0, j * LANES:(j + 1) * LANES, :] = (qs * scale).T.astype(BF16)
    k = jnp.dot(xkv, wkv_ref[:, :D_K], preferred_element_type=F32)
    for j, ks in enumerate(_rope(k, ra, rb, rc)):
        k_ref[:, j * LANES:(j + 1) * LANES] = ks.astype(BF16)
    v = jnp.dot(xkv, wkv_ref[:, D_K:], preferred_element_type=F32)
    ones = jnp.ones((VT_ROWS - DIFF_V_DIM, TK), BF16)
    for t in range(v.shape[0] // TK):
        for j in range(DIFF_HEADS):
            vt_ref[0, t, j * VT_ROWS:j * VT_ROWS + DIFF_V_DIM, :] = (
                v[t * TK:(t + 1) * TK, j * DIFF_V_DIM:(j + 1) * DIFF_V_DIM].T.astype(BF16))
            vt_ref[0, t, j * VT_ROWS + DIFF_V_DIM:(j + 1) * VT_ROWS, :] = ones


def _rope_tables(seq):
    half = ROT_DIM // 2
    inv_freq = jnp.power(ROPE_THETA, -jnp.arange(half, dtype=F32) * 2.0 / ROT_DIM)
    ang = jnp.arange(seq, dtype=jnp.int32).astype(F32)[:, None] * inv_freq[None, :]
    cos, sin = jnp.cos(ang), jnp.sin(ang)
    ones = jnp.ones((seq, DIFF_HEAD_DIM - ROT_DIM), F32)
    zeros_h = jnp.zeros((seq, half), F32)
    zeros_r = jnp.zeros((seq, DIFF_HEAD_DIM - ROT_DIM), F32)
    ra = jnp.concatenate([cos, cos, ones], axis=1)
    rb = jnp.concatenate([-sin, zeros_h, zeros_r], axis=1)
    rc = jnp.concatenate([zeros_h, sin, zeros_r], axis=1)
    rep = LANES // DIFF_HEAD_DIM
    return tuple(jnp.tile(a, (1, rep)) for a in (ra, rb, rc))


def _qkv_proj(h2d, batch, seq, gq, gkv, w_q, w_kv):
    tm = TM_QKV
    assert tm % TK == 0
    nb = seq // tm
    row = lambda b, i: (b * nb + i, 0)
    pos = lambda b, i: (i, 0)
    col = lambda b, i: (b, 0, i)
    ra, rb, rc = _rope_tables(seq)
    return pl.pallas_call(
        _qkv_kernel,
        grid=(batch, nb),
        in_specs=[
            pl.BlockSpec((tm, D_MODEL), row),
            _resident((1, D_MODEL)),
            _resident((1, D_MODEL)),
            _resident((D_MODEL, D_K)),
            _resident((D_MODEL, D_K + D_V)),
            pl.BlockSpec((tm, LANES), pos),
            pl.BlockSpec((tm, LANES), pos),
            pl.BlockSpec((tm, LANES), pos),
        ],
        out_specs=[pl.BlockSpec((1, D_K, tm), col), pl.BlockSpec((tm, D_K), row),
                   pl.BlockSpec((1, tm // TK, DIFF_HEADS * VT_ROWS, TK), lambda b, i: (b, i, 0, 0))],
        out_shape=[jax.ShapeDtypeStruct((batch, D_K, seq), BF16),
                   jax.ShapeDtypeStruct((batch * seq, D_K), BF16),
                   jax.ShapeDtypeStruct((batch, seq // TK, DIFF_HEADS * VT_ROWS, TK), BF16)],
        compiler_params=_params(),
        name="qkv",
    )(h2d, gq, gkv, w_q, w_kv, ra, rb, rc)


def _attn_kernel(qt_ref, qt_next_ref, k_ref, vt_ref, lq_ref, lk_ref, subg_ref, o_ref,
                 qq_ref, qq_next_ref, acc_ref, s_ref, max_ref, *, lam_init):
    i = pl.program_id(2)
    n_half = TQ // TC
    d = DIFF_HEAD_DIM
    dv = DIFF_V_DIM

    all_chains = list(range(2 * n_half))

    zero = jnp.zeros((d, TC), BF16)
    for src, dst in ((qt_ref, qq_ref), (qt_next_ref, qq_next_ref)):
        for r in range(n_half):
            qt = src[0, :, r * TC:(r + 1) * TC]
            dst[r, :d, :] = qt[:d, :]
            dst[r, d:, :] = zero
            dst[n_half + r, :d, :] = zero
            dst[n_half + r, d:, :] = qt[d:, :]
    acc_ref[...] = jnp.zeros(acc_ref.shape, F32)

    key_idx = lax.broadcasted_iota(jnp.int32, (TK, TC), 0)
    qry_idx = lax.broadcasted_iota(jnp.int32, (TK, TC), 1)
    causal = key_idx <= qry_idx

    def chains_from(first_chain_row):
        return [c for c in all_chains if first_chain_row is None or c % n_half >= first_chain_row]

    def scores(j, slot, first_chain_row, stacked_q=qq_ref):
        kj = k_ref[pl.ds(pl.multiple_of(j * TK, TK), TK), :]
        col_max = [None] * len(all_chains)
        for c in chains_from(first_chain_row):
            s = jnp.dot(kj, stacked_q[c], preferred_element_type=F32)
            s_ref[slot, c] = s
            col_max[c] = jnp.max(s, axis=0, keepdims=True)
        return tuple(col_max)

    def softmax_pv(j, slot, ms, col_max, first_chain_row):
        vtj = vt_ref[0, j]
        ms = list(ms)
        for c in chains_from(first_chain_row):
            s = s_ref[slot, c]
            mx = col_max[c]
            if first_chain_row is not None and c % n_half == first_chain_row:
                s = jnp.where(causal, s, -jnp.inf)
                mx = jnp.max(s, axis=0, keepdims=True)
            m_new = jnp.maximum(ms[c], mx)
            alpha = jnp.exp2(ms[c] - m_new)
            p = jnp.exp2(s - m_new).astype(BF16)
            ms[c] = m_new
            pv = jnp.dot(vtj, p, preferred_element_type=F32)
            acc_ref[c] = alpha * acc_ref[c] + pv
        return tuple(ms)

    def earlier_query_block(qb, carry):
        ms, ahead = carry
        ahead = list(ahead)
        b0 = qb * SCORE_SLOTS
        for u in range(SCORE_SLOTS):
            ahead.append(scores(b0 + u + LOOKAHEAD, (u + LOOKAHEAD) % SCORE_SLOTS, None))
            ms = softmax_pv(b0 + u, u, ms, ahead.pop(0), None)
        return ms, tuple(ahead)

    def keep_maxes(u, col_max):
        for c in all_chains:
            max_ref[u, c, 0:1, :] = col_max[c]

    @pl.when(i == 0)
    def _():
        for u in range(LOOKAHEAD):
            keep_maxes(u, scores(u, u, None))

    first_diag = i * SCORE_SLOTS
    ms = tuple(jnp.full((1, TC), -jnp.inf, F32) for _ in all_chains)
    ahead = tuple(tuple(max_ref[u, c, 0:1, :] for c in all_chains) for u in range(LOOKAHEAD))
    ms, ahead = lax.fori_loop(0, i, earlier_query_block, (ms, ahead))
    ahead = list(ahead)
    for r in range(SCORE_SLOTS):
        if r + LOOKAHEAD < SCORE_SLOTS:
            ahead.append(scores(first_diag + r + LOOKAHEAD, r + LOOKAHEAD, r + LOOKAHEAD))
        ms = softmax_pv(first_diag + r, r, ms, ahead.pop(0), r)
        if r < LOOKAHEAD:
            keep_maxes(r, scores(r, r, None, qq_next_ref))

    prod = lq_ref[...] * lk_ref[...]
    e = jnp.exp(jnp.sum(prod, axis=-1, keepdims=True))
    lam = e[0:1, :] - e[1:2, :] + lam_init
    for r in range(n_half):
        c1, c2 = r, n_half + r
        o1 = acc_ref[c1, :dv, :] * (1.0 / acc_ref[c1, dv:dv + 1, :])
        o2 = acc_ref[c2, :dv, :] * (1.0 / acc_ref[c2, dv:dv + 1, :])
        o = (o1 - lam * o2).T
        o = _rms_scale(o) * subg_ref[...]
        o_ref[r * TC:(r + 1) * TC, :] = (o * (1.0 - lam_init)).astype(BF16)


def _diff_attention(qt, k, vt, batch, seq, lam_q, lam_k, sub_g, lam_init):
    assert TC == TK and TQ == SCORE_SLOTS * TK and 0 < LOOKAHEAD < SCORE_SLOTS
    nq = seq // TQ
    return pl.pallas_call(
        functools.partial(_attn_kernel, lam_init=lam_init),
        grid=(batch, DIFF_HEADS, nq),
        in_specs=[
            pl.BlockSpec((1, LANES, TQ), lambda b, h, i: (b, h, i)),
            pl.BlockSpec((1, LANES, TQ), lambda b, h, i: (b, h, jnp.minimum(i + 1, nq - 1))),
            pl.BlockSpec((seq, LANES), lambda b, h, i: (b, h)),
            pl.BlockSpec((1, seq // TK, VT_ROWS, TK), lambda b, h, i: (b, 0, h, 0)),
            _resident((2, DIFF_HEAD_DIM)),
            _resident((2, DIFF_HEAD_DIM)),
            _resident((1, DIFF_V_DIM)),
        ],
        out_specs=pl.BlockSpec((TQ, LANES), lambda b, h, i: (b * nq + i, h)),
        out_shape=jax.ShapeDtypeStruct((batch * seq, D_V), BF16),
        scratch_shapes=[pltpu.VMEM((2 * TQ // TC, LANES, TC), BF16),
                        pltpu.VMEM((2 * TQ // TC, LANES, TC), BF16),
                        pltpu.VMEM((2 * TQ // TC, VT_ROWS, TC), F32),
                        pltpu.VMEM((SCORE_SLOTS, 2 * TQ // TC, TK, TC), F32),
                        pltpu.VMEM((LOOKAHEAD, 2 * TQ // TC, F32_SUBLANES, TC), F32)],
        compiler_params=pltpu.CompilerParams(
            vmem_limit_bytes=VMEM_LIMIT_BYTES,
            dimension_semantics=("arbitrary", "arbitrary", "arbitrary")),
        name="diff_attn",
    )(qt, qt, k, vt, lam_q, lam_k, sub_g)


def kernel(x, attn_norm_g, ffn_norm_g, gmlp_w_in, gmlp_ln_g, gmlp_ln_b, gmlp_w_s, gmlp_b_s, gmlp_w_out, kv_norm_g, w_kv, diff_w_q, diff_lambda_q, diff_lambda_k, diff_sub_g, diff_w_o, ffn_w_gu, ffn_w_down, final_norm_g):
    batch, seq, _ = x.shape
    depth = attn_norm_g.shape[0]
    assert depth == 2 and gmlp_w_in.shape[0] == N_A and diff_w_q.shape[0] == depth - N_A
    row = lambda a: a.reshape(1, -1)
    h = x.reshape(batch * seq, D_MODEL)

    h = _gmlp_layer(h, row(attn_norm_g[0]), gmlp_w_in[0].astype(BF16), row(gmlp_ln_g[0]),
                    row(gmlp_ln_b[0]), gmlp_w_s[0], gmlp_b_s[0].T, gmlp_w_out[0].astype(BF16))
    h = _ffn_layer(h, row(ffn_norm_g[0]), ffn_w_gu[0].astype(BF16), ffn_w_down[0].astype(BF16))

    qt, k, vt = _qkv_proj(h, batch, seq, row(attn_norm_g[1]), row(kv_norm_g),
                          diff_w_q[0].astype(BF16), w_kv.astype(BF16))
    lam_init = 0.8 - 0.6 * math.exp(-0.3 * 1)
    attn = _diff_attention(qt, k, vt, batch, seq, diff_lambda_q[0], diff_lambda_k[0],
                           row(diff_sub_g[0]), lam_init)
    h = _ffn_layer(h, row(ffn_norm_g[1]), ffn_w_gu[1].astype(BF16), ffn_w_down[1].astype(BF16),
                   attn=attn, w_o=diff_w_o[0].astype(BF16), final_g=row(final_norm_g))
    return h.reshape(batch, seq, D_MODEL)
```

```python
import functools
import math

import jax
import jax.numpy as jnp
from jax import lax
from jax.experimental import pallas as pl
from jax.experimental.pallas import tpu as pltpu

D_MODEL = 1024
CHUNK = 128
GMLP_WIDTH = 2 * D_MODEL
GMLP_GROUPS = 8
GMLP_GROUP_DIM = GMLP_WIDTH // GMLP_GROUPS
DIFF_HEADS = 8
DIFF_HEAD_DIM = D_MODEL // (2 * DIFF_HEADS)
DIFF_V_DIM = 2 * DIFF_HEAD_DIM
D_K = DIFF_HEADS * 2 * DIFF_HEAD_DIM
D_V = DIFF_HEADS * DIFF_V_DIM
ROT_DIM = DIFF_HEAD_DIM // 4
ROPE_THETA = 500000.0
D_FF = -(-(8 * D_MODEL) // (3 * 256)) * 256
EPS = 1e-5
N_A = 1

LANES = 128
VMEM_LIMIT_BYTES = 56 * 2**20

TM_GMLP = 512
TM_FFN = 512
TM_QKV = 1024
TQ = 1024
TK = 256
TC = 256
SCORE_SLOTS = TQ // TK
LOOKAHEAD = 1
BF16_SUBLANES = 16
F32_SUBLANES = 8
VT_ROWS = DIFF_V_DIM + BF16_SUBLANES

F32 = jnp.float32
BF16 = jnp.bfloat16


def _rms_scale(x):
    return x * lax.rsqrt(jnp.mean(x * x, axis=-1, keepdims=True) + EPS)


def _gelu(x):
    return 0.5 * x * (1.0 + lax.erf(x * math.sqrt(0.5)))


def _resident(shape):
    return pl.BlockSpec(shape, lambda *_: (0,) * len(shape), pipeline_mode=pl.Buffered(1))


def _params():
    return pltpu.CompilerParams(vmem_limit_bytes=VMEM_LIMIT_BYTES)


def _gmlp_kernel(x_ref, g_ref, win_ref, lng_ref, lnb_ref, ws_ref, bs_ref, wout_ref, o_ref,
                 vn_ref, gated_ref):
    x = x_ref[...]
    xn = (_rms_scale(x) * g_ref[...]).astype(BF16)

    v = jnp.dot(xn, win_ref[:, GMLP_WIDTH:], preferred_element_type=F32)
    v = _gelu(v)
    mu = jnp.mean(v, axis=-1, keepdims=True)
    vc = v - mu
    vn = vc * lax.rsqrt(jnp.mean(vc * vc, axis=-1, keepdims=True) + EPS)
    vn_ref[...] = (vn * lng_ref[...] + lnb_ref[...]).astype(BF16)

    row = lax.broadcasted_iota(jnp.int32, (CHUNK, CHUNK), 0)
    col = lax.broadcasted_iota(jnp.int32, (CHUNK, CHUNK), 1)
    causal = col <= row
    n_chunks = x.shape[0] // CHUNK
    for g in range(GMLP_GROUPS):
        cols = slice(g * GMLP_GROUP_DIM, (g + 1) * GMLP_GROUP_DIM)
        u = jnp.dot(xn, win_ref[:, cols], preferred_element_type=F32)
        u = _gelu(u)
        ws = jnp.where(causal, ws_ref[g], 0.0).astype(BF16)
        bias = bs_ref[:, g:g + 1]
        for c in range(n_chunks):
            rows = slice(c * CHUNK, (c + 1) * CHUNK)
            mixed = jnp.dot(ws, vn_ref[rows, cols], preferred_element_type=F32) + bias
            gated_ref[rows, cols] = (u[rows, :] * mixed).astype(BF16)

    o_ref[...] = x + jnp.dot(gated_ref[...], wout_ref[...], preferred_element_type=F32)


def _gmlp_layer(x2d, norm_g, w_in, ln_g, ln_b, w_s, b_s_t, w_out):
    t = x2d.shape[0]
    tm = TM_GMLP
    row = lambda i: (i, 0)
    return pl.pallas_call(
        _gmlp_kernel,
        grid=(t // tm,),
        in_specs=[
            pl.BlockSpec((tm, D_MODEL), row),
            _resident((1, D_MODEL)),
            _resident((D_MODEL, 2 * GMLP_WIDTH)),
            _resident((1, GMLP_WIDTH)),
            _resident((1, GMLP_WIDTH)),
            _resident((GMLP_GROUPS, CHUNK, CHUNK)),
            _resident((CHUNK, GMLP_GROUPS)),
            _resident((GMLP_WIDTH, D_MODEL)),
        ],
        out_specs=pl.BlockSpec((tm, D_MODEL), row),
        out_shape=jax.ShapeDtypeStruct((t, D_MODEL), F32),
        scratch_shapes=[pltpu.VMEM((tm, GMLP_WIDTH), BF16), pltpu.VMEM((tm, GMLP_WIDTH), BF16)],
        compiler_params=_params(),
        name="gmlp",
    )(x2d, norm_g, w_in, ln_g, ln_b, w_s, b_s_t, w_out)


def _ffn_kernel(*refs, has_oproj, has_final_norm):
    refs = list(refs)
    h_ref = refs.pop(0)
    if has_oproj:
        a_ref = refs.pop(0)
        wo_ref = refs.pop(0)
    g_ref, wgu_ref, wd_ref = refs.pop(0), refs.pop(0), refs.pop(0)
    if has_final_norm:
        fg_ref = refs.pop(0)
    (o_ref,) = refs

    h = h_ref[...]
    if has_oproj:
        h = h + jnp.dot(a_ref[...], wo_ref[...], preferred_element_type=F32)
    xn = (_rms_scale(h) * g_ref[...]).astype(BF16)
    gate = jnp.dot(xn, wgu_ref[:, :D_FF], preferred_element_type=F32)
    up = jnp.dot(xn, wgu_ref[:, D_FF:], preferred_element_type=F32)
    act = (jax.nn.silu(gate) * up).astype(BF16)
    h = h + jnp.dot(act, wd_ref[...], preferred_element_type=F32)
    if has_final_norm:
        h = _rms_scale(h) * fg_ref[...]
    o_ref[...] = h


def _ffn_layer(h2d, norm_g, w_gu, w_down, attn=None, w_o=None, final_g=None):
    t = h2d.shape[0]
    tm = TM_FFN
    row = lambda i: (i, 0)
    has_oproj = attn is not None
    has_final_norm = final_g is not None
    args = [h2d]
    specs = [pl.BlockSpec((tm, D_MODEL), row)]
    if has_oproj:
        args += [attn, w_o]
        specs += [pl.BlockSpec((tm, D_V), row), _resident((D_V, D_MODEL))]
    args += [norm_g, w_gu, w_down]
    specs += [_resident((1, D_MODEL)), _resident((D_MODEL, 2 * D_FF)), _resident((D_FF, D_MODEL))]
    if has_final_norm:
        args.append(final_g)
        specs.append(_resident((1, D_MODEL)))
    return pl.pallas_call(
        functools.partial(_ffn_kernel, has_oproj=has_oproj, has_final_norm=has_final_norm),
        grid=(t // tm,),
        in_specs=specs,
        out_specs=pl.BlockSpec((tm, D_MODEL), row),
        out_shape=jax.ShapeDtypeStruct((t, D_MODEL), F32),
        compiler_params=_params(),
        name="ffn_final" if has_final_norm else "ffn",
    )(*args)


def _rope(x, ra, rb, rc):
    half = ROT_DIM // 2
    outs = []
    for j in range(x.shape[1] // LANES):
        xs = x[:, j * LANES:(j + 1) * LANES]
        fwd = pltpu.roll(xs, LANES - half, axis=1)
        bwd = pltpu.roll(xs, half, axis=1)
        outs.append(xs * ra + fwd * rb + bwd * rc)
    return outs


def _qkv_kernel(h_ref, gq_ref, gkv_ref, wq_ref, wkv_ref, ra_ref, rb_ref, rc_ref,
                qt_ref, k_ref, vt_ref):
    y = _rms_scale(h_ref[...])
    xq = (y * gq_ref[...]).astype(BF16)
    xkv = (y * gkv_ref[...]).astype(BF16)
    ra, rb, rc = ra_ref[...], rb_ref[...], rc_ref[...]
    q = jnp.dot(xq, wq_ref[...], preferred_element_type=F32)
    scale = DIFF_HEAD_DIM ** -0.5 * math.log2(math.e)
    for j, qs in enumerate(_rope(q, ra, rb, rc)):
        qt_ref[0, j * LANES:(j + 1) * LANES, :] = (qs * scale).T.astype(BF16)
    k = jnp.dot(xkv, wkv_ref[:, :D_K], preferred_element_type=F32)
    for j, ks in enumerate(_rope(k, ra, rb, rc)):
        k_ref[:, j * LANES:(j + 1) * LANES] = ks.astype(BF16)
    v = jnp.dot(xkv, wkv_ref[:, D_K:], preferred_element_type=F32)
    ones = jnp.ones((VT_ROWS - DIFF_V_DIM, TK), BF16)
    for t in range(v.shape[0] // TK):
        for j in range(DIFF_HEADS):
            vt_ref[0, t, j * VT_ROWS:j * VT_ROWS + DIFF_V_DIM, :] = (
                v[t * TK:(t + 1) * TK, j * DIFF_V_DIM:(j + 1) * DIFF_V_DIM].T.astype(BF16))
            vt_ref[0, t, j * VT_ROWS + DIFF_V_DIM:(j + 1) * VT_ROWS, :] = ones


def _rope_tables(seq):
    half = ROT_DIM // 2
    inv_freq = jnp.power(ROPE_THETA, -jnp.arange(half, dtype=F32) * 2.0 / ROT_DIM)
    ang = jnp.arange(seq, dtype=jnp.int32).astype(F32)[:, None] * inv_freq[None, :]
    cos, sin = jnp.cos(ang), jnp.sin(ang)
    ones = jnp.ones((seq, DIFF_HEAD_DIM - ROT_DIM), F32)
    zeros_h = jnp.zeros((seq, half), F32)
    zeros_r = jnp.zeros((seq, DIFF_HEAD_DIM - ROT_DIM), F32)
    ra = jnp.concatenate([cos, cos, ones], axis=1)
    rb = jnp.concatenate([-sin, zeros_h, zeros_r], axis=1)
    rc = jnp.concatenate([zeros_h, sin, zeros_r], axis=1)
    rep = LANES // DIFF_HEAD_DIM
    return tuple(jnp.tile(a, (1, rep)) for a in (ra, rb, rc))


def _qkv_proj(h2d, batch, seq, gq, gkv, w_q, w_kv):
    tm = TM_QKV
    assert tm % TK == 0
    nb = seq // tm
    row = lambda b, i: (b * nb + i, 0)
    pos = lambda b, i: (i, 0)
    col = lambda b, i: (b, 0, i)
    ra, rb, rc = _rope_tables(seq)
    return pl.pallas_call(
        _qkv_kernel,
        grid=(batch, nb),
        in_specs=[
            pl.BlockSpec((tm, D_MODEL), row),
            _resident((1, D_MODEL)),
            _resident((1, D_MODEL)),
            _resident((D_MODEL, D_K)),
            _resident((D_MODEL, D_K + D_V)),
            pl.BlockSpec((tm, LANES), pos),
            pl.BlockSpec((tm, LANES), pos),
            pl.BlockSpec((tm, LANES), pos),
        ],
        out_specs=[pl.BlockSpec((1, D_K, tm), col), pl.BlockSpec((tm, D_K), row),
                   pl.BlockSpec((1, tm // TK, DIFF_HEADS * VT_ROWS, TK), lambda b, i: (b, i, 0, 0))],
        out_shape=[jax.ShapeDtypeStruct((batch, D_K, seq), BF16),
                   jax.ShapeDtypeStruct((batch * seq, D_K), BF16),
                   jax.ShapeDtypeStruct((batch, seq // TK, DIFF_HEADS * VT_ROWS, TK), BF16)],
        compiler_params=_params(),
        name="qkv",
    )(h2d, gq, gkv, w_q, w_kv, ra, rb, rc)


def _attn_kernel(qt_ref, qt_next_ref, k_ref, vt_ref, lq_ref, lk_ref, subg_ref, o_ref,
                 qq_ref, qq_next_ref, acc_ref, s_ref, max_ref, *, lam_init):
    i = pl.program_id(2)
    n_half = TQ // TC
    d = DIFF_HEAD_DIM
    dv = DIFF_V_DIM

    all_chains = list(range(2 * n_half))

    zero = jnp.zeros((d, TC), BF16)
    for src, dst in ((qt_ref, qq_ref), (qt_next_ref, qq_next_ref)):
        for r in range(n_half):
            qt = src[0, :, r * TC:(r + 1) * TC]
            dst[r, :d, :] = qt[:d, :]
            dst[r, d:, :] = zero
            dst[n_half + r, :d, :] = zero
            dst[n_half + r, d:, :] = qt[d:, :]
    acc_ref[...] = jnp.zeros(acc_ref.shape, F32)

    key_idx = lax.broadcasted_iota(jnp.int32, (TK, TC), 0)
    qry_idx = lax.broadcasted_iota(jnp.int32, (TK, TC), 1)
    causal = key_idx <= qry_idx

    def chains_from(first_chain_row):
        return [c for c in all_chains if first_chain_row is None or c % n_half >= first_chain_row]

    def scores(j, slot, first_chain_row, stacked_q=qq_ref):
        kj = k_ref[pl.ds(pl.multiple_of(j * TK, TK), TK), :]
        col_max = [None] * len(all_chains)
        for c in chains_from(first_chain_row):
            s = jnp.dot(kj, stacked_q[c], preferred_element_type=F32)
            s_ref[slot, c] = s
            col_max[c] = jnp.max(s, axis=0, keepdims=True)
        return tuple(col_max)

    def softmax_pv(j, slot, ms, col_max, first_chain_row):
        vtj = vt_ref[0, j]
        ms = list(ms)
        for c in chains_from(first_chain_row):
            s = s_ref[slot, c]
            mx = col_max[c]
            if first_chain_row is not None and c % n_half == first_chain_row:
                s = jnp.where(causal, s, -jnp.inf)
                mx = jnp.max(s, axis=0, keepdims=True)
            m_new = jnp.maximum(ms[c], mx)
            alpha = jnp.exp2(ms[c] - m_new)
            p = jnp.exp2(s - m_new).astype(BF16)
            ms[c] = m_new
            pv = jnp.dot(vtj, p, preferred_element_type=F32)
            acc_ref[c] = alpha * acc_ref[c] + pv
        return tuple(ms)

    def earlier_query_block(qb, carry):
        ms, ahead = carry
        ahead = list(ahead)
        b0 = qb * SCORE_SLOTS
        for u in range(SCORE_SLOTS):
            ahead.append(scores(b0 + u + LOOKAHEAD, (u + LOOKAHEAD) % SCORE_SLOTS, None))
            ms = softmax_pv(b0 + u, u, ms, ahead.pop(0), None)
        return ms, tuple(ahead)

    def keep_maxes(u, col_max):
        for c in all_chains:
            max_ref[u, c, 0:1, :] = col_max[c]

    @pl.when(i == 0)
    def _():
        for u in range(LOOKAHEAD):
            keep_maxes(u, scores(u, u, None))

    first_diag = i * SCORE_SLOTS
    ms = tuple(jnp.full((1, TC), -jnp.inf, F32) for _ in all_chains)
    ahead = tuple(tuple(max_ref[u, c, 0:1, :] for c in all_chains) for u in range(LOOKAHEAD))
    ms, ahead = lax.fori_loop(0, i, earlier_query_block, (ms, ahead))
    ahead = list(ahead)
    for r in range(SCORE_SLOTS):
        if r + LOOKAHEAD < SCORE_SLOTS:
            ahead.append(scores(first_diag + r + LOOKAHEAD, r + LOOKAHEAD, r + LOOKAHEAD))
        ms = softmax_pv(first_diag + r, r, ms, ahead.pop(0), r)
        if r < LOOKAHEAD:
            keep_maxes(r, scores(r, r, None, qq_next_ref))

    prod = lq_ref[...] * lk_ref[...]
    e = jnp.exp(jnp.sum(prod, axis=-1, keepdims=True))
    lam = e[0:1, :] - e[1:2, :] + lam_init
    for r in range(n_half):
        c1, c2 = r, n_half + r
        o1 = acc_ref[c1, :dv, :] * (1.0 / acc_ref[c1, dv:dv + 1, :])
        o2 = acc_ref[c2, :dv, :] * (1.0 / acc_ref[c2, dv:dv + 1, :])
        o = (o1 - lam * o2).T
        o = _rms_scale(o) * subg_ref[...]
        o_ref[r * TC:(r + 1) * TC, :] = (o * (1.0 - lam_init)).astype(BF16)


def _diff_attention(qt, k, vt, batch, seq, lam_q, lam_k, sub_g, lam_init):
    assert TC == TK and TQ == SCORE_SLOTS * TK and 0 < LOOKAHEAD < SCORE_SLOTS
    nq = seq // TQ
    return pl.pallas_call(
        functools.partial(_attn_kernel, lam_init=lam_init),
        grid=(batch, DIFF_HEADS, nq),
        in_specs=[
            pl.BlockSpec((1, LANES, TQ), lambda b, h, i: (b, h, i)),
            pl.BlockSpec((1, LANES, TQ), lambda b, h, i: (b, h, jnp.minimum(i + 1, nq - 1))),
            pl.BlockSpec((seq, LANES), lambda b, h, i: (b, h)),
            pl.BlockSpec((1, seq // TK, VT_ROWS, TK), lambda b, h, i: (b, 0, h, 0)),
            _resident((2, DIFF_HEAD_DIM)),
            _resident((2, DIFF_HEAD_DIM)),
            _resident((1, DIFF_V_DIM)),
        ],
        out_specs=pl.BlockSpec((TQ, LANES), lambda b, h, i: (b * nq + i, h)),
        out_shape=jax.ShapeDtypeStruct((batch * seq, D_V), BF16),
        scratch_shapes=[pltpu.VMEM((2 * TQ // TC, LANES, TC), BF16),
                        pltpu.VMEM((2 * TQ // TC, LANES, TC), BF16),
                        pltpu.VMEM((2 * TQ // TC, VT_ROWS, TC), F32),
                        pltpu.VMEM((SCORE_SLOTS, 2 * TQ // TC, TK, TC), F32),
                        pltpu.VMEM((LOOKAHEAD, 2 * TQ // TC, F32_SUBLANES, TC), F32)],
        compiler_params=pltpu.CompilerParams(
            vmem_limit_bytes=VMEM_LIMIT_BYTES,
            dimension_semantics=("arbitrary", "arbitrary", "arbitrary")),
        name="diff_attn",
    )(qt, qt, k, vt, lam_q, lam_k, sub_g)


def kernel(x, attn_norm_g, ffn_norm_g, gmlp_w_in, gmlp_ln_g, gmlp_ln_b, gmlp_w_s, gmlp_b_s, gmlp_w_out, kv_norm_g, w_kv, diff_w_q, diff_lambda_q, diff_lambda_k, diff_sub_g, diff_w_o, ffn_w_gu, ffn_w_down, final_norm_g):
    batch, seq, _ = x.shape
    depth = attn_norm_g.shape[0]
    assert depth == 2 and gmlp_w_in.shape[0] == N_A and diff_w_q.shape[0] == depth - N_A
    row = lambda a: a.reshape(1, -1)
    h = x.reshape(batch * seq, D_MODEL)

    h = _gmlp_layer(h, row(attn_norm_g[0]), gmlp_w_in[0].astype(BF16), row(gmlp_ln_g[0]),
                    row(gmlp_ln_b[0]), gmlp_w_s[0], gmlp_b_s[0].T, gmlp_w_out[0].astype(BF16))
    h = _ffn_layer(h, row(ffn_norm_g[0]), ffn_w_gu[0].astype(BF16), ffn_w_down[0].astype(BF16))

    qt, k, vt = _qkv_proj(h, batch, seq, row(attn_norm_g[1]), row(kv_norm_g),
                          diff_w_q[0].astype(BF16), w_kv.astype(BF16))
    lam_init = 0.8 - 0.6 * math.exp(-0.3 * 1)
    attn = _diff_attention(qt, k, vt, batch, seq, diff_lambda_q[0], diff_lambda_k[0],
                           row(diff_sub_g[0]), lam_init)
    h = _ffn_layer(h, row(ffn_norm_g[1]), ffn_w_gu[1].astype(BF16), ffn_w_down[1].astype(BF16),
                   attn=attn, w_o=diff_w_o[0].astype(BF16), final_g=row(final_norm_g))
    return h.reshape(batch, seq, D_MODEL)
```
